```python
import jax, jax.numpy as jnp
from jax import lax
import numpy as np

D_MODEL = 2048
BATCH = 8
SEQ = 4096
DEPTH = 2
DEC_BATCH = 32
DEC_SEQ = 64
PAST_LEN = 4096

CHUNK = 64
N_EVEN = (DEPTH + 1) // 2
N_ODD = DEPTH // 2
DH = 128
D_A = D_MODEL // 2
CONV_A = 3
H_B = 8
D_B = H_B * DH
FGATE_BIAS_INIT = 3.0
FOX_BLOCK = 128
D_C = D_MODEL // 2
CONV_C = 31
D_D = D_MODEL // 2
D_GROUPS = 8
D_GROUP_W = D_D // D_GROUPS
MLP_CHUNK = 128
N_MEM = 256
H_MEM = 4
D_XMEM = H_MEM * DH
N_GROUPS = 4
EXP_PER_GROUP = 8
N_EXPERTS = N_GROUPS * EXP_PER_GROUP
TOP_K = 2
D_EXPERT = 512
MOE_BLOCK = 128
EPS = 1e-6

kernel_name = 'hybrid_streaming_encoder_step'


def rmsnorm(x, g):
    xf = x.astype(jnp.float32)
    xf = xf * lax.rsqrt(jnp.mean(xf * xf, axis=-1, keepdims=True) + EPS)
    return (xf * g.astype(jnp.float32)).astype(x.dtype)


def layernorm(x, g, b):
    xf = x.astype(jnp.float32)
    xc = xf - jnp.mean(xf, axis=-1, keepdims=True)
    var = jnp.mean(xc * xc, axis=-1, keepdims=True)
    return (xc * lax.rsqrt(var + EPS) * g.astype(jnp.float32) + b.astype(jnp.float32)).astype(x.dtype)


def causal_dwconv(u, prev, w):
    full = jnp.concatenate([prev.astype(u.dtype), u], axis=1)
    y = lax.conv_general_dilated(full, w[:, None, :].astype(u.dtype), window_strides=(1,), padding='VALID',
                                 dimension_numbers=('NWC', 'WIO', 'NWC'), feature_group_count=u.shape[-1])
    return y, full[:, full.shape[1] - (w.shape[0] - 1):]


def fox_attend(q, k, v, c_q, c_k, q_pos, k_pos):
    s = jnp.einsum('bqhd,bkhd->bhqk', q, k, preferred_element_type=jnp.float32) * (DH ** -0.5)
    bias = jnp.transpose(c_q, (0, 2, 1))[:, :, :, None] - jnp.transpose(c_k, (0, 2, 1))[:, :, None, :]
    mask = k_pos[None, :] <= q_pos[:, None]
    p = jax.nn.softmax(jnp.where(mask, s + bias, -jnp.inf), axis=-1)
    return jnp.einsum('bhqk,bkhd->bqhd', p.astype(v.dtype), v)


def fox_prompt(q, k, v, logf):
    B, S, H, _ = q.shape
    c = jnp.cumsum(logf, axis=1)
    nb = S // FOX_BLOCK
    qb = jnp.transpose(q.reshape(B, nb, FOX_BLOCK, H, DH), (1, 0, 2, 3, 4))
    cb = jnp.transpose(c.reshape(B, nb, FOX_BLOCK, H), (1, 0, 2, 3))
    k_pos = jnp.arange(S)

    def block(args):
        i, qi, ci = args
        q_pos = i * FOX_BLOCK + jnp.arange(FOX_BLOCK)
        return fox_attend(qi, k, v, ci, c, q_pos, k_pos)

    out = lax.map(block, (jnp.arange(nb), qb, cb))
    return jnp.transpose(out, (1, 0, 2, 3, 4)).reshape(B, S, H, DH)


def fox_sample(q, k, v, logf, ck, cv, clf):
    P = ck.shape[1]
    T = q.shape[1]
    k_all = jnp.concatenate([ck.astype(k.dtype), k], axis=1)
    v_all = jnp.concatenate([cv.astype(v.dtype), v], axis=1)
    c = jnp.cumsum(jnp.concatenate([clf.astype(jnp.float32), logf], axis=1), axis=1)
    return fox_attend(q, k_all, v_all, c[:, P:], c, P + jnp.arange(T), jnp.arange(P + T))


def even_mixer(xn, conv_prev, fox_past, w_in, w_conv_a, q_gain, k_gain, f_bias, w_out):
    B, T, _ = xn.shape
    sizes = [D_A, D_A, D_A, D_B, D_B, D_B, H_B]
    idx = [int(s) for s in np.cumsum(sizes)[:-1]]
    a_b, a_c, a_h, q, k, v, f_logit = jnp.split(xn @ w_in, idx, axis=-1)
    conv, conv_state = causal_dwconv(a_c * a_h, conv_prev, w_conv_a)
    y_a = a_b * conv
    q = rmsnorm(q.reshape(B, T, H_B, DH), q_gain)
    k = rmsnorm(k.reshape(B, T, H_B, DH), k_gain)
    v = v.reshape(B, T, H_B, DH)
    logf = jax.nn.log_sigmoid(f_logit.astype(jnp.float32) + f_bias.astype(jnp.float32))
    if fox_past is None:
        y_b = fox_prompt(q, k, v, logf)
    else:
        y_b = fox_sample(q, k, v, logf, fox_past[0], fox_past[1], fox_past[2])
    y = jnp.concatenate([y_a, y_b.reshape(B, T, D_B)], axis=-1) @ w_out
    return y, (conv_state, k, v, logf)


def spatial_gate(v, w_s, b_s):
    B, T, _ = v.shape
    pad = (-T) % MLP_CHUNK
    nc = (T + pad) // MLP_CHUNK
    vp = jnp.pad(v, ((0, 0), (0, pad), (0, 0))).reshape(B, nc, MLP_CHUNK, D_GROUPS, D_GROUP_W)
    tril = jnp.tril(jnp.ones((MLP_CHUNK, MLP_CHUNK), dtype=bool))
    w = jnp.where(tril[None], w_s, 0).astype(v.dtype)
    out = jnp.einsum('gts,bnsgc->bntgc', w, vp) + jnp.transpose(b_s, (1, 0))[None, None, :, :, None].astype(v.dtype)
    return out.reshape(B, nc * MLP_CHUNK, D_D)[:, :T]


def odd_mixer(xn, conv_prev, w_in, w_conv_c, ln_g, ln_b, v_gain, w_s, b_s, w_out):
    idx = [D_C, 2 * D_C, 2 * D_C + D_D]
    c_val, c_gate, u, v = jnp.split(xn @ w_in, idx, axis=-1)
    g = c_val * jax.nn.sigmoid(c_gate)
    conv, conv_state = causal_dwconv(g, conv_prev, w_conv_c)
    y_c = jax.nn.silu(layernorm(conv, ln_g, ln_b))
    vn = rmsnorm(v, v_gain)
    y_d = u * spatial_gate(vn, w_s, b_s)
    y = jnp.concatenate([y_c, y_d], axis=-1) @ w_out
    return y, (conv_state, vn)


def mem_kv(mem, g_mem, w_k, w_v, k_gain):
    B = mem.shape[0]
    mn = rmsnorm(mem, g_mem)
    k = rmsnorm((mn @ w_k).reshape(B, N_MEM, H_MEM, DH), k_gain)
    v = (mn @ w_v).reshape(B, N_MEM, H_MEM, DH)
    return k, v


def mem_attend(xn, k, v, w_q, q_gain, w_o):
    B, T, _ = xn.shape
    q = rmsnorm((xn @ w_q).reshape(B, T, H_MEM, DH), q_gain)
    s = jnp.einsum('bthd,bmhd->bhtm', q, k.astype(q.dtype), preferred_element_type=jnp.float32) * (DH ** -0.5)
    p = jax.nn.softmax(s, axis=-1).astype(q.dtype)
    o = jnp.einsum('bhtm,bmhd->bthd', p, v.astype(q.dtype))
    return o.reshape(B, T, D_XMEM) @ w_o


def hier_moe(x, w_group, b_group, w_router, b_router, w_gate, w_up, w_down):
    T, D = x.shape
    lg = (x @ w_group).astype(jnp.float32) + b_group.astype(jnp.float32)
    grp = jnp.argmax(lg, axis=-1)
    p_sel = jnp.take_along_axis(jax.nn.softmax(lg, axis=-1), grp[:, None], axis=1)
    le = ((x @ w_router).astype(jnp.float32) + b_router.astype(jnp.float32)).reshape(T, N_GROUPS, EXP_PER_GROUP)
    le = jnp.take_along_axis(le, grp[:, None, None], axis=1)[:, 0]
    top_v, top_i = lax.top_k(le, TOP_K)
    gates = (jax.nn.softmax(top_v, axis=-1) * p_sel).astype(x.dtype)
    eid = grp[:, None] * EXP_PER_GROUP + top_i
    A = T * TOP_K
    e_flat = eid.reshape(A)
    tok_flat = jnp.repeat(jnp.arange(T, dtype=jnp.int32), TOP_K)
    w_flat = gates.reshape(A)
    order = jnp.argsort(e_flat, stable=True)
    e_s, tok_s, w_s = e_flat[order], tok_flat[order], w_flat[order]
    counts = jnp.zeros((N_EXPERTS,), jnp.int32).at[e_flat].add(1)
    starts = jnp.cumsum(counts) - counts
    padded = ((counts + MOE_BLOCK - 1) // MOE_BLOCK) * MOE_BLOCK
    pend = jnp.cumsum(padded)
    pstarts = pend - padded
    pos = pstarts[e_s] + (jnp.arange(A, dtype=jnp.int32) - starts[e_s])
    n_blocks = (A + N_EXPERTS * (MOE_BLOCK - 1) + MOE_BLOCK - 1) // MOE_BLOCK
    P = n_blocks * MOE_BLOCK
    buf_tok = jnp.zeros((P,), jnp.int32).at[pos].set(tok_s)
    buf_w = jnp.zeros((P,), x.dtype).at[pos].set(w_s)
    blk_e = jnp.minimum(jnp.searchsorted(pend, jnp.arange(n_blocks, dtype=jnp.int32) * MOE_BLOCK, side='right'),
                        N_EXPERTS - 1)
    xb = x[buf_tok].reshape(n_blocks, MOE_BLOCK, D)

    def expert_block(args):
        xi, e = args
        h = jax.nn.silu(xi @ w_gate[e]) * (xi @ w_up[e])
        return h @ w_down[e]

    yb = lax.map(expert_block, (xb, blk_e)).reshape(P, D)
    return jax.ops.segment_sum(yb * buf_w[:, None], buf_tok, num_segments=T)


def moe_layer(h, w_group, b_group, w_router, b_router, w_gate, w_up, w_down):
    B, T, D = h.shape
    return hier_moe(h.reshape(B * T, D), w_group, b_group, w_router, b_router, w_gate, w_up, w_down).reshape(B, T, D)


def setup_inputs(seed: int = 0) -> dict:
    key = jax.random.key(seed)
    keys = iter(jax.random.split(key, 64))
    f32 = jnp.float32

    def nrm(shape, scale=1.0):
        return jax.random.normal(next(keys), shape, f32) * scale

    def gain(shape):
        return 1.0 + 0.05 * nrm(shape)

    d_in_even = 3 * D_A + 3 * D_B + H_B
    d_in_odd = 2 * D_C + 2 * D_D
    return {
        'x_prompt': nrm((BATCH, SEQ, D_MODEL)),
        'x_sample': nrm((DEC_BATCH, DEC_SEQ, D_MODEL)),
        'mem_prompt': nrm((BATCH, N_MEM, D_MODEL)),
        'cache_fox_k': nrm((N_EVEN, DEC_BATCH, PAST_LEN, H_B, DH)),
        'cache_fox_v': nrm((N_EVEN, DEC_BATCH, PAST_LEN, H_B, DH)),
        'cache_fox_logf': jax.nn.log_sigmoid(FGATE_BIAS_INIT + nrm((N_EVEN, DEC_BATCH, PAST_LEN, H_B))),
        'state_conv_a': nrm((N_EVEN, DEC_BATCH, CONV_A - 1, D_A)),
        'state_conv_c': nrm((N_ODD, DEC_BATCH, CONV_C - 1, D_C)),
        'cache_mem_k': nrm((DEPTH, DEC_BATCH, N_MEM, H_MEM, DH)),
        'cache_mem_v': nrm((DEPTH, DEC_BATCH, N_MEM, H_MEM, DH)),
        'norm_mix': gain((DEPTH, D_MODEL)),
        'norm_xmem': gain((DEPTH, D_MODEL)),
        'norm_mem': gain((DEPTH, D_MODEL)),
        'norm_ffn': gain((DEPTH, D_MODEL)),
        'even_w_in': nrm((N_EVEN, D_MODEL, d_in_even), D_MODEL ** -0.5),
        'even_w_conv_a': nrm((N_EVEN, CONV_A, D_A), CONV_A ** -0.5),
        'fox_q_gain': gain((N_EVEN, DH)),
        'fox_k_gain': gain((N_EVEN, DH)),
        'fox_f_bias': FGATE_BIAS_INIT + 0.3 * nrm((N_EVEN, H_B)),
        'even_w_out': nrm((N_EVEN, D_A + D_B, D_MODEL), (D_A + D_B) ** -0.5),
        'odd_w_in': nrm((N_ODD, D_MODEL, d_in_odd), D_MODEL ** -0.5),
        'odd_w_conv_c': nrm((N_ODD, CONV_C, D_C), CONV_C ** -0.5),
        'conf_ln_gain': gain((N_ODD, D_C)),
        'conf_ln_bias': 0.02 * nrm((N_ODD, D_C)),
        'sgu_v_gain': gain((N_ODD, D_D)),
        'sgu_w': nrm((N_ODD, D_GROUPS, MLP_CHUNK, MLP_CHUNK), MLP_CHUNK ** -0.5),
        'sgu_b': 1.0 + 0.05 * nrm((N_ODD, D_GROUPS, MLP_CHUNK)),
        'odd_w_out': nrm((N_ODD, D_C + D_D, D_MODEL), (D_C + D_D) ** -0.5),
        'xmem_wq': nrm((DEPTH, D_MODEL, D_XMEM), D_MODEL ** -0.5),
        'xmem_wk': nrm((DEPTH, D_MODEL, D_XMEM), D_MODEL ** -0.5),
        'xmem_wv': nrm((DEPTH, D_MODEL, D_XMEM), D_MODEL ** -0.5),
        'xmem_q_gain': gain((DEPTH, DH)),
        'xmem_k_gain': gain((DEPTH, DH)),
        'xmem_wo': nrm((DEPTH, D_XMEM, D_MODEL), D_XMEM ** -0.5),
        'moe_w_group': nrm((DEPTH, D_MODEL, N_GROUPS), D_MODEL ** -0.5),
        'moe_b_group': 0.01 * nrm((DEPTH, N_GROUPS)),
        'moe_w_router': nrm((DEPTH, D_MODEL, N_EXPERTS), D_MODEL ** -0.5),
        'moe_b_router': 0.01 * nrm((DEPTH, N_EXPERTS)),
        'moe_w_gate': nrm((DEPTH, N_EXPERTS, D_MODEL, D_EXPERT), D_MODEL ** -0.5),
        'moe_w_up': nrm((DEPTH, N_EXPERTS, D_MODEL, D_EXPERT), D_MODEL ** -0.5),
        'moe_w_down': nrm((DEPTH, N_EXPERTS, D_EXPERT, D_MODEL), D_EXPERT ** -0.5),
    }


def reference(x_prompt, x_sample, mem_prompt, cache_fox_k, cache_fox_v, cache_fox_logf,
              state_conv_a, state_conv_c, cache_mem_k, cache_mem_v,
              norm_mix, norm_xmem, norm_mem, norm_ffn,
              even_w_in, even_w_conv_a, fox_q_gain, fox_k_gain, fox_f_bias, even_w_out,
              odd_w_in, odd_w_conv_c, conf_ln_gain, conf_ln_bias, sgu_v_gain, sgu_w, sgu_b, odd_w_out,
              xmem_wq, xmem_wk, xmem_wv, xmem_q_gain, xmem_k_gain, xmem_wo,
              moe_w_group, moe_b_group, moe_w_router, moe_b_router, moe_w_gate, moe_w_up, moe_w_down):
    xp, xs = x_prompt, x_sample
    bp = xp.shape[0]
    fk_p, fv_p, fl_p, fk_s, fv_s, fl_s = [], [], [], [], [], []
    ca_p, ca_s, cc_p, cc_s, cv_s, mk_p, mv_p = [], [], [], [], [], [], []
    for i in range(DEPTH):
        hp = rmsnorm(xp, norm_mix[i])
        hs = rmsnorm(xs, norm_mix[i])
        if i % 2 == 0:
            e = i // 2
            wts = (even_w_in[e], even_w_conv_a[e], fox_q_gain[e], fox_k_gain[e], fox_f_bias[e], even_w_out[e])
            yp, (sa_p, kp, vp, lp) = even_mixer(hp, jnp.zeros((bp, CONV_A - 1, D_A), xp.dtype), None, *wts)
            ys, (sa_s, ks, vs, ls) = even_mixer(hs, state_conv_a[e], (cache_fox_k[e], cache_fox_v[e], cache_fox_logf[e]), *wts)
            fk_p.append(kp); fv_p.append(vp); fl_p.append(lp)
            fk_s.append(ks); fv_s.append(vs); fl_s.append(ls)
            ca_p.append(sa_p); ca_s.append(sa_s)
        else:
            o = i // 2
            wts = (odd_w_in[o], odd_w_conv_c[o], conf_ln_gain[o], conf_ln_bias[o], sgu_v_gain[o], sgu_w[o], sgu_b[o], odd_w_out[o])
            yp, (sc_p, _vn_p) = odd_mixer(hp, jnp.zeros((bp, CONV_C - 1, D_C), xp.dtype), *wts)
            ys, (sc_s, vn_s) = odd_mixer(hs, state_conv_c[o], *wts)
            cc_p.append(sc_p); cc_s.append(sc_s); cv_s.append(vn_s)
        xp = xp + yp
        xs = xs + ys
        mk, mv = mem_kv(mem_prompt, norm_mem[i], xmem_wk[i], xmem_wv[i], xmem_k_gain[i])
        mk_p.append(mk); mv_p.append(mv)
        xp = xp + mem_attend(rmsnorm(xp, norm_xmem[i]), mk, mv, xmem_wq[i], xmem_q_gain[i], xmem_wo[i])
        xs = xs + mem_attend(rmsnorm(xs, norm_xmem[i]), cache_mem_k[i], cache_mem_v[i], xmem_wq[i], xmem_q_gain[i], xmem_wo[i])
        moe_w = (moe_w_group[i], moe_b_group[i], moe_w_router[i], moe_b_router[i], moe_w_gate[i], moe_w_up[i], moe_w_down[i])
        xp = xp + moe_layer(rmsnorm(xp, norm_ffn[i]), *moe_w)
        xs = xs + moe_layer(rmsnorm(xs, norm_ffn[i]), *moe_w)
    return (xp, xs,
            jnp.stack(fk_p), jnp.stack(fv_p), jnp.stack(fl_p),
            jnp.stack(fk_s), jnp.stack(fv_s), jnp.stack(fl_s),
            jnp.stack(ca_p), jnp.stack(ca_s),
            jnp.stack(cc_p), jnp.stack(cc_s),
            jnp.stack(cv_s),
            jnp.stack(mk_p), jnp.stack(mv_p))
```

```python
import functools

import jax
import jax.numpy as jnp
from jax import lax
from jax.experimental import pallas as pl
from jax.experimental.pallas import tpu as pltpu

F32 = jnp.float32
BF16 = jnp.bfloat16
I32 = jnp.int32

EPS = 1e-6
DH = 128
LANES = 128
CHUNK_MLP = 128
N_GROUPS = 4
EXP_PER_GROUP = 8
N_EXPERTS = N_GROUPS * EXP_PER_GROUP
V7X_VMEM_LIMIT = 56 * 1024 * 1024
ROW_TILE = 512
ATT_TILE = 512
CACHE_TILE = 512
GATHER_TILE = 256


def _cparams(*sem):
    return pltpu.CompilerParams(dimension_semantics=sem, vmem_limit_bytes=V7X_VMEM_LIMIT)


def _rms(x, g):
    return x * lax.rsqrt(jnp.mean(x * x, axis=-1, keepdims=True) + EPS) * g


def _row_tile(n_rows, seq_len, cap=ROW_TILE):
    if seq_len >= cap:
        assert seq_len % cap == 0
        return cap
    nb = max(1, cap // seq_len)
    n_seq = n_rows // seq_len
    while n_seq % nb:
        nb -= 1
    return nb * seq_len


def _even_in_kernel(x_ref, g_ref, w_ref, wf_ref, fb_ref, wc_ref, qg_ref, kg_ref, prev_ref,
                    ya_ref, q_ref, k_ref, kb_ref, v_ref, vb_ref, lf_ref, st_ref,
                    xn_s, ab_s, ac_s, ext_s, *, nb, lt, tiles_per_seq, n_heads):
    i = pl.program_id(0)
    j = pl.program_id(1)
    halo = 8

    @pl.when(j == 0)
    def _():
        xn = _rms(x_ref[...], g_ref[...]).astype(BF16)
        xn_s[...] = xn
        z = jnp.dot(xn, wf_ref[...], preferred_element_type=F32) + fb_ref[...]
        lf_ref[...] = jnp.minimum(z, 0.0) - jnp.log1p(jnp.exp(-jnp.abs(z)))

    acc = jnp.dot(xn_s[...], w_ref[...], preferred_element_type=F32)

    @pl.when(j == 0)
    def _():
        ab_s[...] = acc

    @pl.when(j == 1)
    def _():
        ac_s[...] = acc

    @pl.when(j == 2)
    def _():
        gated = ac_s[...] * acc
        for n in range(nb):
            ext_s[n, halo:halo + lt, :] = gated[n * lt:(n + 1) * lt, :]

        @pl.when(i % tiles_per_seq == 0)
        def _():
            for n in range(nb):
                ext_s[n, halo - 2:halo, :] = prev_ref[n]

        w = wc_ref[...]
        for n in range(nb):
            conv = (w[0:1, :] * ext_s[n, halo - 2:halo - 2 + lt, :]
                    + w[1:2, :] * ext_s[n, halo - 1:halo - 1 + lt, :]
                    + w[2:3, :] * ext_s[n, halo:halo + lt, :])
            ya_ref[n * lt:(n + 1) * lt, :] = (ab_s[n * lt:(n + 1) * lt, :] * conv).astype(BF16)
            tail = ext_s[n, halo - 2 + lt:halo + lt, :]
            st_ref[n] = tail
            ext_s[n, halo - 2:halo, :] = tail

    def head_norm(gain):
        for h in range(n_heads):
            seg = acc[:, h * DH:(h + 1) * DH]
            yield h, seg * lax.rsqrt(jnp.mean(seg * seg, axis=-1, keepdims=True) + EPS) * gain

    @pl.when(j == 3)
    def _():
        for h, qn in head_norm(qg_ref[...]):
            q_ref[:, h * DH:(h + 1) * DH] = (qn * (DH ** -0.5)).astype(BF16)

    @pl.when(j == 4)
    def _():
        for h, kn in head_norm(kg_ref[...]):
            k_ref[:, h * DH:(h + 1) * DH] = kn
            kb_ref[:, h * DH:(h + 1) * DH] = kn.astype(BF16)

    @pl.when(j == 5)
    def _():
        v_ref[...] = acc
        vb_ref[...] = acc.astype(BF16)


def _even_in(x, g, w_main, w_f, f_bias, w_conv, q_gain, k_gain, prev, seq_len):
    t, d = x.shape
    dc = w_conv.shape[1]
    n_col = w_main.shape[1] // dc
    assert n_col == 6
    n_heads = dc // DH
    tm = _row_tile(t, seq_len)
    lt = min(seq_len, tm)
    nb = tm // lt
    tiles_per_seq = seq_len // lt
    n_seq = t // seq_len
    kern = functools.partial(_even_in_kernel, nb=nb, lt=lt, tiles_per_seq=tiles_per_seq, n_heads=n_heads)
    row = lambda i, j: (i, 0)
    const = lambda i, j: (0, 0)
    seq3 = lambda i, j: (i // tiles_per_seq, 0, 0)
    return pl.pallas_call(
        kern,
        grid=(t // tm, n_col),
        in_specs=[
            pl.BlockSpec((tm, d), row),
            pl.BlockSpec((1, d), const),
            pl.BlockSpec((d, dc), lambda i, j: (0, j)),
            pl.BlockSpec((d, LANES), const),
            pl.BlockSpec((1, LANES), const),
            pl.BlockSpec(w_conv.shape, const),
            pl.BlockSpec((1, DH), const),
            pl.BlockSpec((1, DH), const),
            pl.BlockSpec((nb, 2, dc), seq3),
        ],
        out_specs=[
            pl.BlockSpec((tm, dc), row),
            pl.BlockSpec((tm, dc), row),
            pl.BlockSpec((tm, dc), row),
            pl.BlockSpec((tm, dc), row),
            pl.BlockSpec((tm, dc), row),
            pl.BlockSpec((tm, dc), row),
            pl.BlockSpec((tm, LANES), row),
            pl.BlockSpec((nb, 2, dc), seq3),
        ],
        out_shape=[
            jax.ShapeDtypeStruct((t, dc), BF16),
            jax.ShapeDtypeStruct((t, dc), BF16),
            jax.ShapeDtypeStruct((t, dc), F32),
            jax.ShapeDtypeStruct((t, dc), BF16),
            jax.ShapeDtypeStruct((t, dc), F32),
            jax.ShapeDtypeStruct((t, dc), BF16),
            jax.ShapeDtypeStruct((t, LANES), F32),
            jax.ShapeDtypeStruct((n_seq, 2, dc), F32),
        ],
        scratch_shapes=[
            pltpu.VMEM((tm, d), BF16),
            pltpu.VMEM((tm, dc), F32),
            pltpu.VMEM((tm, dc), F32),
            pltpu.VMEM((nb, 8 + lt, dc), F32),
        ],
        compiler_params=_cparams("arbitrary", "arbitrary"),
        name="even_in",
    )(x, g, w_main, w_f, f_bias, w_conv, q_gain, k_gain, prev)


def _cumsum_kernel(lf_ref, c_ref, carry_s, *, ts):
    @pl.when(pl.program_id(1) == 0)
    def _():
        carry_s[...] = jnp.zeros_like(carry_s)

    x = lf_ref[0]
    row = lax.broadcasted_iota(I32, x.shape, 0)
    d = 1
    while d < ts:
        x = x + jnp.where(row >= d, pltpu.roll(x, d, axis=0), 0.0)
        d *= 2
    x = x + carry_s[...]
    c_ref[0] = x
    carry_s[...] = x[ts - 1:ts, :]


def _time_tile(s, cap=1024):
    ts = min(s, cap)
    while s % ts or ts % 8:
        ts -= 1
    return ts


def _cumsum_time(lf):
    b, s, _ = lf.shape
    ts = _time_tile(s)
    return pl.pallas_call(
        functools.partial(_cumsum_kernel, ts=ts),
        grid=(b, s // ts),
        in_specs=[pl.BlockSpec((1, ts, LANES), lambda bi, si: (bi, si, 0))],
        out_specs=pl.BlockSpec((1, ts, LANES), lambda bi, si: (bi, si, 0)),
        out_shape=jax.ShapeDtypeStruct(lf.shape, F32),
        scratch_shapes=[pltpu.VMEM((1, LANES), F32)],
        compiler_params=_cparams("arbitrary", "arbitrary"),
        name="cumsum_time",
    )(lf)


def _online_softmax_step(s, v, m, l, acc):
    m_new = jnp.maximum(m, jnp.max(s, axis=1, keepdims=True))
    p = jnp.exp(s - m_new)
    alpha = jnp.exp(m - m_new)
    l = alpha * l + jnp.sum(p, axis=1, keepdims=True)
    acc = alpha * acc + jnp.dot(p.astype(BF16), v, preferred_element_type=F32)
    return m_new, l, acc


def _fox_prompt_kernel(q_ref, k_ref, v_ref, cq_ref, ck_ref, o_ref, *, tq):
    h = pl.program_id(1)
    qi = pl.program_id(2)
    q = q_ref[0]
    lane = lax.broadcasted_iota(I32, (tq, LANES), 1)
    cq = jnp.sum(jnp.where(lane == h, cq_ref[0], 0.0), axis=1, keepdims=True)

    def scores(kj):
        ks = pl.multiple_of(kj * tq, tq)
        k = k_ref[0, pl.ds(ks, tq), :]
        v = v_ref[0, pl.ds(ks, tq), :]
        s = lax.dot_general(q, k, (((1,), (1,)), ((), ())), preferred_element_type=F32)
        return s + (cq - ck_ref[0, 0, :, pl.ds(ks, tq)]), v

    def past_block(kj, carry):
        s, v = scores(kj)
        return _online_softmax_step(s, v, *carry)

    init = (jnp.full((tq, 1), -jnp.inf, F32), jnp.zeros((tq, 1), F32), jnp.zeros((tq, DH), F32))
    m, l, acc = lax.fori_loop(0, qi, past_block, init)
    s, v = scores(qi)
    r = lax.broadcasted_iota(I32, (tq, tq), 0)
    c = lax.broadcasted_iota(I32, (tq, tq), 1)
    s = jnp.where(c <= r, s, -jnp.inf)
    m, l, acc = _online_softmax_step(s, v, m, l, acc)
    o_ref[0] = (acc / l).astype(BF16)


def _fox_prompt(q, k, v, c_pad, c_rows):
    b, s, hd = q.shape
    n_heads = hd // DH
    tq = min(ATT_TILE, s)
    assert s % tq == 0
    return pl.pallas_call(
        functools.partial(_fox_prompt_kernel, tq=tq),
        grid=(b, n_heads, s // tq),
        in_specs=[
            pl.BlockSpec((1, tq, DH), lambda bi, h, qi: (bi, qi, h)),
            pl.BlockSpec((1, s, DH), lambda bi, h, qi: (bi, 0, h)),
            pl.BlockSpec((1, s, DH), lambda bi, h, qi: (bi, 0, h)),
            pl.BlockSpec((1, tq, LANES), lambda bi, h, qi: (bi, qi, 0)),
            pl.BlockSpec((1, 1, 1, s), lambda bi, h, qi: (bi, h, 0, 0)),
        ],
        out_specs=pl.BlockSpec((1, tq, DH), lambda bi, h, qi: (bi, qi, h)),
        out_shape=jax.ShapeDtypeStruct((b, s, hd), BF16),
        compiler_params=_cparams("arbitrary", "arbitrary", "arbitrary"),
        name="fox_prompt",
    )(q, k, v, c_pad, c_rows)


def _fox_sample_kernel(q_ref, ck_ref, cv_ref, kn_ref, vn_ref, cq_ref, cr_ref, o_ref,
                       m_s, l_s, acc_s, *, n_heads, t, tk, past):
    kj = pl.program_id(1)
    last = pl.num_programs(1) - 1

    @pl.when(kj == 0)
    def _():
        m_s[...] = jnp.full_like(m_s, -jnp.inf)
        l_s[...] = jnp.zeros_like(l_s)
        acc_s[...] = jnp.zeros_like(acc_s)

    lane = lax.broadcasted_iota(I32, (t, LANES), 1)
    ks = pl.multiple_of(kj * tk, tk)
    for h in range(n_heads):
        sl = slice(h * DH, (h + 1) * DH)
        q = q_ref[0, :, sl]
        cq = jnp.sum(jnp.where(lane == h, cq_ref[0], 0.0), axis=1, keepdims=True)
        k = ck_ref[0, :, sl].astype(BF16)
        v = cv_ref[0, :, sl].astype(BF16)
        s = lax.dot_general(q, k, (((1,), (1,)), ((), ())), preferred_element_type=F32)
        s = s + (cq - cr_ref[0, h, :, pl.ds(ks, tk)])
        m, l, acc = _online_softmax_step(s, v, m_s[h], l_s[h], acc_s[h])

        @pl.when(kj != last)
        def _():
            m_s[h] = m
            l_s[h] = l
            acc_s[h] = acc

        @pl.when(kj == last)
        def _():
            s2 = lax.dot_general(q, kn_ref[0, :, sl], (((1,), (1,)), ((), ())), preferred_element_type=F32)
            s2 = s2 + (cq - cr_ref[0, h, :, past:past + t])
            r = lax.broadcasted_iota(I32, (t, t), 0)
            c = lax.broadcasted_iota(I32, (t, t), 1)
            s2 = jnp.where(c <= r, s2, -jnp.inf)
            _, l2, acc2 = _online_softmax_step(s2, vn_ref[0, :, sl], m, l, acc)
            o_ref[0, :, sl] = (acc2 / l2).astype(BF16)


def _fox_sample(q, cache_k, cache_v, k_new, v_new, c_pad, c_rows):
    b, t, hd = q.shape
    n_heads = hd // DH
    past = cache_k.shape[1]
    tk = min(CACHE_TILE, past)
    assert past % tk == 0 and past % t == 0 and tk % LANES == 0
    kern = functools.partial(_fox_sample_kernel, n_heads=n_heads, t=t, tk=tk, past=past)
    per_b = lambda bi, kj: (bi, 0, 0)
    return pl.pallas_call(
        kern,
        grid=(b, past // tk),
        in_specs=[
            pl.BlockSpec((1, t, hd), per_b),
            pl.BlockSpec((1, tk, hd), lambda bi, kj: (bi, kj, 0)),
            pl.BlockSpec((1, tk, hd), lambda bi, kj: (bi, kj, 0)),
            pl.BlockSpec((1, t, hd), per_b),
            pl.BlockSpec((1, t, hd), per_b),
            pl.BlockSpec((1, t, LANES), lambda bi, kj: (bi, past // t, 0)),
            pl.BlockSpec((1, n_heads, 1, past + t), lambda bi, kj: (bi, 0, 0, 0)),
        ],
        out_specs=pl.BlockSpec((1, t, hd), per_b),
        out_shape=jax.ShapeDtypeStruct((b, t, hd), BF16),
        scratch_shapes=[
            pltpu.VMEM((n_heads, t, 1), F32),
            pltpu.VMEM((n_heads, t, 1), F32),
            pltpu.VMEM((n_heads, t, DH), F32),
        ],
        compiler_params=_cparams("arbitrary", "arbitrary"),
        name="fox_sample",
    )(q, cache_k, cache_v, k_new, v_new, c_pad, c_rows)


def _out_proj_kernel(a_ref, b_ref, wa_ref, wb_ref, r_ref, o_ref):
    o_ref[...] = (r_ref[...]
                  + jnp.dot(a_ref[...], wa_ref[...], preferred_element_type=F32)
                  + jnp.dot(b_ref[...], wb_ref[...], preferred_element_type=F32))


def _out_proj(a, b, w, res):
    t, ka = a.shape
    kb = b.shape[1]
    d = w.shape[1]
    tm = _row_tile(t, t)
    assert ka == kb
    return pl.pallas_call(
        _out_proj_kernel,
        grid=(t // tm,),
        in_specs=[
            pl.BlockSpec((tm, ka), lambda i: (i, 0)),
            pl.BlockSpec((tm, kb), lambda i: (i, 0)),
            pl.BlockSpec((ka, d), lambda i: (0, 0)),
            pl.BlockSpec((kb, d), lambda i: (1, 0)),
            pl.BlockSpec((tm, d), lambda i: (i, 0)),
        ],
        out_specs=pl.BlockSpec((tm, d), lambda i: (i, 0)),
        out_shape=jax.ShapeDtypeStruct((t, d), F32),
        compiler_params=_cparams("arbitrary"),
        name="out_proj",
    )(a, b, w, w, res)


def _odd_in_kernel(x_ref, g_ref, w_ref, wc_ref, lg_ref, lb_ref, vg_ref, ws_ref, bs_ref, prev_ref,
                   yc_ref, yd_ref, vn_ref, st_ref,
                   xn_s, val_s, u_s, ext_s, *, nb, lt, tiles_per_seq, taps, n_groups, lc):
    i = pl.program_id(0)
    j = pl.program_id(1)
    hist = taps - 1
    halo = ((hist + 7) // 8) * 8

    @pl.when(j == 0)
    def _():
        xn_s[...] = _rms(x_ref[...], g_ref[...]).astype(BF16)

    acc = jnp.dot(xn_s[...], w_ref[...], preferred_element_type=F32)

    @pl.when(j == 0)
    def _():
        val_s[...] = acc

    @pl.when(j == 1)
    def _():
        glu = val_s[...] * jax.nn.sigmoid(acc)
        for n in range(nb):
            ext_s[n, halo:halo + lt, :] = glu[n * lt:(n + 1) * lt, :]

        @pl.when(i % tiles_per_seq == 0)
        def _():
            for n in range(nb):
                ext_s[n, halo - hist:halo, :] = prev_ref[n]

        for n in range(nb):
            conv = wc_ref[0:1, :] * ext_s[n, halo - hist:halo - hist + lt, :]
            for s in range(1, taps):
                conv = conv + wc_ref[s:s + 1, :] * ext_s[n, halo - hist + s:halo - hist + s + lt, :]
            mu = jnp.mean(conv, axis=-1, keepdims=True)
            xc = conv - mu
            var = jnp.mean(xc * xc, axis=-1, keepdims=True)
            y = xc * lax.rsqrt(var + EPS) * lg_ref[...] + lb_ref[...]
            yc_ref[n * lt:(n + 1) * lt, :] = (y * jax.nn.sigmoid(y)).astype(BF16)
            tail = ext_s[n, halo - hist + lt:halo + lt, :]
            st_ref[n] = tail
            ext_s[n, halo - hist:halo, :] = tail

    @pl.when(j == 2)
    def _():
        u_s[...] = acc

    @pl.when(j == 3)
    def _():
        vn = _rms(acc, vg_ref[...])
        vn_ref[...] = vn
        vb = vn.astype(BF16)
        gw = vn.shape[1] // n_groups
        for c in range(vn.shape[0] // lc):
            rows = slice(c * lc, (c + 1) * lc)
            for g in range(n_groups):
                cols = slice(g * gw, (g + 1) * gw)
                gate = jnp.dot(ws_ref[g], vb[rows, cols], preferred_element_type=F32) + bs_ref[g]
                yd_ref[rows, cols] = (u_s[rows, cols] * gate).astype(BF16)


def _odd_in(x, g, w_in, w_conv, ln_g, ln_b, v_gain, w_s, b_s, prev, seq_len):
    t, d = x.shape
    taps, dc = w_conv.shape
    assert w_in.shape[1] == 4 * dc
    n_groups = w_s.shape[0]
    tm = _row_tile(t, seq_len, cap=ROW_TILE if seq_len >= ROW_TILE else ROW_TILE // 2)
    lt = min(seq_len, tm)
    nb = tm // lt
    tiles_per_seq = seq_len // lt
    n_seq = t // seq_len
    lc = min(CHUNK_MLP, seq_len)
    assert lt % lc == 0
    hist = taps - 1
    halo = ((hist + 7) // 8) * 8
    kern = functools.partial(_odd_in_kernel, nb=nb, lt=lt, tiles_per_seq=tiles_per_seq, taps=taps,
                             n_groups=n_groups, lc=lc)
    row = lambda i, j: (i, 0)
    const = lambda i, j: (0, 0)
    const3 = lambda i, j: (0, 0, 0)
    seq3 = lambda i, j: (i // tiles_per_seq, 0, 0)
    return pl.pallas_call(
        kern,
        grid=(t // tm, 4),
        in_specs=[
            pl.BlockSpec((tm, d), row),
            pl.BlockSpec((1, d), const),
            pl.BlockSpec((d, dc), lambda i, j: (0, j)),
            pl.BlockSpec((taps, dc), const),
            pl.BlockSpec((1, dc), const),
            pl.BlockSpec((1, dc), const),
            pl.BlockSpec((1, dc), const),
            pl.BlockSpec((n_groups, lc, lc), const3),
            pl.BlockSpec((n_groups, lc, 1), const3),
            pl.BlockSpec((nb, hist, dc), seq3),
        ],
        out_specs=[
            pl.BlockSpec((tm, dc), row),
            pl.BlockSpec((tm, dc), row),
            pl.BlockSpec((tm, dc), row),
            pl.BlockSpec((nb, hist, dc), seq3),
        ],
        out_shape=[
            jax.ShapeDtypeStruct((t, dc), BF16),
            jax.ShapeDtypeStruct((t, dc), BF16),
            jax.ShapeDtypeStruct((t, dc), F32),
            jax.ShapeDtypeStruct((n_seq, hist, dc), F32),
        ],
        scratch_shapes=[
            pltpu.VMEM((tm, d), BF16),
            pltpu.VMEM((tm, dc), F32),
            pltpu.VMEM((tm, dc), F32),
            pltpu.VMEM((nb, halo + lt, dc), F32),
        ],
        compiler_params=_cparams("arbitrary", "arbitrary"),
        name="odd_in",
    )(x, g, w_in, w_conv, ln_g, ln_b, v_gain, w_s, b_s, prev)


def _mem_kv_kernel(m_ref, g_ref, wk_ref, wv_ref, kg_ref, k_ref, v_ref, *, n_heads):
    mn = _rms(m_ref[...], g_ref[...]).astype(BF16)
    k = jnp.dot(mn, wk_ref[...], preferred_element_type=F32)
    for h in range(n_heads):
        seg = k[:, h * DH:(h + 1) * DH]
        k_ref[:, h * DH:(h + 1) * DH] = seg * lax.rsqrt(jnp.mean(seg * seg, axis=-1, keepdims=True) + EPS) * kg_ref[...]
    v_ref[...] = jnp.dot(mn, wv_ref[...], preferred_element_type=F32)


def _mem_kv(mem, g, wk, wv, k_gain):
    t, d = mem.shape
    dx = wk.shape[1]
    tm = _row_tile(t, t)
    return pl.pallas_call(
        functools.partial(_mem_kv_kernel, n_heads=dx // DH),
        grid=(t // tm,),
        in_specs=[
            pl.BlockSpec((tm, d), lambda i: (i, 0)),
            pl.BlockSpec((1, d), lambda i: (0, 0)),
            pl.BlockSpec((d, dx), lambda i: (0, 0)),
            pl.BlockSpec((d, dx), lambda i: (0, 0)),
            pl.BlockSpec((1, DH), lambda i: (0, 0)),
        ],
        out_specs=[pl.BlockSpec((tm, dx), lambda i: (i, 0))] * 2,
        out_shape=[jax.ShapeDtypeStruct((t, dx), F32)] * 2,
        compiler_params=_cparams("arbitrary"),
        name="mem_kv",
    )(mem, g, wk, wv, k_gain)


def _mem_attn_kernel(x_ref, g_ref, wq_ref, qg_ref, k_ref, v_ref, wo_ref, o_ref, att_s, *, nb, lt, n_heads):
    x = x_ref[...]
    xn = _rms(x, g_ref[...]).astype(BF16)
    q = jnp.dot(xn, wq_ref[...], preferred_element_type=F32)
    for h in range(n_heads):
        sl = slice(h * DH, (h + 1) * DH)
        seg = q[:, sl]
        qn = seg * lax.rsqrt(jnp.mean(seg * seg, axis=-1, keepdims=True) + EPS) * qg_ref[...]
        qb = (qn * (DH ** -0.5)).astype(BF16)
        for n in range(nb):
            rows = slice(n * lt, (n + 1) * lt)
            k = k_ref[n, :, sl].astype(BF16)
            v = v_ref[n, :, sl].astype(BF16)
            s = lax.dot_general(qb[rows], k, (((1,), (1,)), ((), ())), preferred_element_type=F32)
            p = jnp.exp(s - jnp.max(s, axis=1, keepdims=True))
            o = jnp.dot(p.astype(BF16), v, preferred_element_type=F32) / jnp.sum(p, axis=1, keepdims=True)
            att_s[rows, sl] = o.astype(BF16)
    o_ref[...] = x + jnp.dot(att_s[...], wo_ref[...], preferred_element_type=F32)


def _mem_attn(x, g, wq, q_gain, k, v, wo, seq_len):
    t, d = x.shape
    dx = wq.shape[1]
    n_mem = k.shape[1]
    tm = _row_tile(t, seq_len, cap=256 if seq_len < 256 else ROW_TILE)
    lt = min(seq_len, tm)
    nb = tm // lt
    tiles_per_seq = seq_len // lt
    kern = functools.partial(_mem_attn_kernel, nb=nb, lt=lt, n_heads=dx // DH)
    return pl.pallas_call(
        kern,
        grid=(t // tm,),
        in_specs=[
            pl.BlockSpec((tm, d), lambda i: (i, 0)),
            pl.BlockSpec((1, d), lambda i: (0, 0)),
            pl.BlockSpec((d, dx), lambda i: (0, 0)),
            pl.BlockSpec((1, DH), lambda i: (0, 0)),
            pl.BlockSpec((nb, n_mem, dx), lambda i: (i // tiles_per_seq, 0, 0)),
            pl.BlockSpec((nb, n_mem, dx), lambda i: (i // tiles_per_seq, 0, 0)),
            pl.BlockSpec((dx, d), lambda i: (0, 0)),
        ],
        out_specs=pl.BlockSpec((tm, d), lambda i: (i, 0)),
        out_shape=jax.ShapeDtypeStruct((t, d), F32),
        scratch_shapes=[pltpu.VMEM((tm, dx), BF16)],
        compiler_params=_cparams("arbitrary"),
        name="mem_attn",
    )(x, g, wq, q_gain, k, v, wo)


GROUP_LANE0 = N_EXPERTS
META_E, META_G, META_R = 0, 2, 4


def _router_kernel(x_ref, g_ref, w_ref, b_ref, meta_ref, cnt_ref, *, tm):
    @pl.when(pl.program_id(0) == 0)
    def _():
        cnt_ref[...] = jnp.zeros_like(cnt_ref)

    xn = _rms(x_ref[...], g_ref[...]).astype(BF16)
    logits = jnp.dot(xn, w_ref[...], preferred_element_type=F32) + b_ref[...]
    lane = lax.broadcasted_iota(I32, (tm, LANES), 1)
    big = jnp.int32(LANES)

    def first_argmax(vals):
        top = jnp.max(vals, axis=1, keepdims=True)
        return top, jnp.min(jnp.where(vals == top, lane, big), axis=1, keepdims=True)

    is_grp = (lane >= GROUP_LANE0) & (lane < GROUP_LANE0 + N_GROUPS)
    lg = jnp.where(is_grp, logits, -jnp.inf)
    g_top, g_lane = first_argmax(lg)
    p_sel = 1.0 / jnp.sum(jnp.where(is_grp, jnp.exp(lg - g_top), 0.0), axis=1, keepdims=True)
    e0 = (g_lane - GROUP_LANE0) * EXP_PER_GROUP
    le = jnp.where((lane >= e0) & (lane < e0 + EXP_PER_GROUP), logits, -jnp.inf)
    v1, i1 = first_argmax(le)
    v2, i2 = first_argmax(jnp.where(lane == i1, -jnp.inf, le))
    e = jnp.exp(v2 - v1)
    g1 = p_sel / (1.0 + e)
    g2 = p_sel * e / (1.0 + e)

    sel1 = lane == i1
    sel2 = lane == i2
    onehot = jnp.where(sel1 | sel2, 1.0, 0.0)
    r = lax.broadcasted_iota(I32, (tm, tm), 0)
    c = lax.broadcasted_iota(I32, (tm, tm), 1)
    earlier = jnp.where(c < r, 1.0, 0.0).astype(BF16)
    before = jnp.dot(earlier, onehot.astype(BF16), preferred_element_type=F32) + cnt_ref[...]
    r1 = jnp.sum(jnp.where(sel1, before, 0.0), axis=1, keepdims=True)
    r2 = jnp.sum(jnp.where(sel2, before, 0.0), axis=1, keepdims=True)
    cnt_ref[...] = cnt_ref[...] + jnp.sum(onehot, axis=0, keepdims=True)

    meta = jnp.zeros((tm, LANES), F32)
    for k, val in ((META_E, i1.astype(F32)), (META_E + 1, i2.astype(F32)), (META_G, g1), (META_G + 1, g2),
                   (META_R, r1), (META_R + 1, r2)):
        meta = jnp.where(lane == k, val, meta)
    meta_ref[...] = meta


def _router(x, g, w_rt, b_rt):
    t, d = x.shape
    tm = _row_tile(t, t)
    return pl.pallas_call(
        functools.partial(_router_kernel, tm=tm),
        grid=(t // tm,),
        in_specs=[
            pl.BlockSpec((tm, d), lambda i: (i, 0)),
            pl.BlockSpec((1, d), lambda i: (0, 0)),
            pl.BlockSpec((d, LANES), lambda i: (0, 0)),
            pl.BlockSpec((1, LANES), lambda i: (0, 0)),
        ],
        out_specs=[pl.BlockSpec((tm, LANES), lambda i: (i, 0)), pl.BlockSpec((1, LANES), lambda i: (0, 0))],
        out_shape=[jax.ShapeDtypeStruct((t, LANES), F32), jax.ShapeDtypeStruct((1, LANES), F32)],
        compiler_params=_cparams("arbitrary"),
        name="moe_router",
    )(x, g, w_rt, b_rt)


def _row_copy(src, src_row, dst, dst_row, sem):
    return pltpu.make_async_copy(src.at[pl.ds(src_row, 1)], dst.at[pl.ds(dst_row, 1)], sem)


def _dispatch_kernel(pos_ref, x_ref, g_ref, zero_ref, xb_ref, xn_s, sem, *, tm):
    del zero_ref
    base = pl.program_id(0) * (2 * tm)
    xn_s[...] = _rms(x_ref[...], g_ref[...])

    def issue(r, _):
        _row_copy(xn_s, r, xb_ref, pos_ref[base + 2 * r], sem).start()
        _row_copy(xn_s, r, xb_ref, pos_ref[base + 2 * r + 1], sem).start()
        return 0

    lax.fori_loop(0, tm, issue, 0)

    def drain(r, _):
        _row_copy(xn_s, 0, xb_ref, 0, sem).wait()
        _row_copy(xn_s, 0, xb_ref, 0, sem).wait()
        return 0

    lax.fori_loop(0, tm, drain, 0)


def _dispatch(pos, x, g, n_rows):
    t, d = x.shape
    tm = min(GATHER_TILE, t)
    assert t % tm == 0
    zeros = jnp.zeros((n_rows, d), F32)
    return pl.pallas_call(
        functools.partial(_dispatch_kernel, tm=tm),
        grid_spec=pltpu.PrefetchScalarGridSpec(
            num_scalar_prefetch=1,
            grid=(t // tm,),
            in_specs=[
                pl.BlockSpec((tm, d), lambda i, pos: (i, 0)),
                pl.BlockSpec((1, d), lambda i, pos: (0, 0)),
                pl.BlockSpec(memory_space=pl.ANY),
            ],
            out_specs=pl.BlockSpec(memory_space=pl.ANY),
            scratch_shapes=[pltpu.VMEM((tm, d), F32), pltpu.SemaphoreType.DMA],
        ),
        out_shape=jax.ShapeDtypeStruct((n_rows, d), F32),
        input_output_aliases={3: 0},
        compiler_params=_cparams("arbitrary"),
        name="moe_dispatch",
    )(pos, x, g, zeros)


def _expert_kernel(be_ref, nu_ref, x_ref, wg_ref, wu_ref, wd_ref, y_ref, wg_s, wu_s, wd_s):
    b = pl.program_id(0)
    used = b < nu_ref[0]
    prev = be_ref[jnp.maximum(b - 1, 0)]

    @pl.when(used & ((b == 0) | (be_ref[b] != prev)))
    def _():
        wg_s[...] = wg_ref[0].astype(BF16)
        wu_s[...] = wu_ref[0].astype(BF16)
        wd_s[...] = wd_ref[0].astype(BF16)

    @pl.when(used)
    def _():
        x = x_ref[...].astype(BF16)
        gate = jnp.dot(x, wg_s[...], preferred_element_type=F32)
        up = jnp.dot(x, wu_s[...], preferred_element_type=F32)
        hid = (gate * jax.nn.sigmoid(gate) * up).astype(BF16)
        y_ref[...] = jnp.dot(hid, wd_s[...], preferred_element_type=F32)

    @pl.when(jnp.logical_not(used))
    def _():
        y_ref[...] = jnp.zeros_like(y_ref)


def _experts(blk_e, n_used, xb, w_gate, w_up, w_down, bm):
    p, d = xb.shape
    de = w_gate.shape[2]
    n_blocks = p // bm

    def xmap(b, be, nu):
        return (jnp.minimum(b, jnp.maximum(nu[0] - 1, 0)), 0)

    def wmap(b, be, nu):
        return (be[b], 0, 0)

    return pl.pallas_call(
        _expert_kernel,
        grid_spec=pltpu.PrefetchScalarGridSpec(
            num_scalar_prefetch=2,
            grid=(n_blocks,),
            in_specs=[
                pl.BlockSpec((bm, d), xmap),
                pl.BlockSpec((1, d, de), wmap),
                pl.BlockSpec((1, d, de), wmap),
                pl.BlockSpec((1, de, d), wmap),
            ],
            out_specs=pl.BlockSpec((bm, d), lambda b, be, nu: (b, 0)),
            scratch_shapes=[pltpu.VMEM((d, de), BF16), pltpu.VMEM((d, de), BF16), pltpu.VMEM((de, d), BF16)],
        ),
        out_shape=jax.ShapeDtypeStruct((p, d), F32),
        compiler_params=_cparams("arbitrary"),
        name="moe_experts",
    )(blk_e, n_used, xb, w_gate, w_up, w_down)


def _combine_kernel(pos_ref, x_ref, meta_ref, yb_ref, o_ref, buf_s, sem, *, tm):
    base = pl.program_id(0) * (2 * tm)

    def issue(r, _):
        _row_copy(yb_ref, pos_ref[base + 2 * r], buf_s.at[0], r, sem).start()
        _row_copy(yb_ref, pos_ref[base + 2 * r + 1], buf_s.at[1], r, sem).start()
        return 0

    lax.fori_loop(0, tm, issue, 0)

    def drain(r, _):
        _row_copy(yb_ref, 0, buf_s.at[0], 0, sem).wait()
        _row_copy(yb_ref, 0, buf_s.at[1], 0, sem).wait()
        return 0

    lax.fori_loop(0, tm, drain, 0)
    meta = meta_ref[...]
    g1 = meta[:, META_G:META_G + 1]
    g2 = meta[:, META_G + 1:META_G + 2]
    o_ref[...] = x_ref[...] + (buf_s[0] * g1 + buf_s[1] * g2)


def _combine(pos, x, meta, yb):
    t, d = x.shape
    tm = min(GATHER_TILE, t)
    return pl.pallas_call(
        functools.partial(_combine_kernel, tm=tm),
        grid_spec=pltpu.PrefetchScalarGridSpec(
            num_scalar_prefetch=1,
            grid=(t // tm,),
            in_specs=[
                pl.BlockSpec((tm, d), lambda i, pos: (i, 0)),
                pl.BlockSpec((tm, LANES), lambda i, pos: (i, 0)),
                pl.BlockSpec(memory_space=pl.ANY),
            ],
            out_specs=pl.BlockSpec((tm, d), lambda i, pos: (i, 0)),
            scratch_shapes=[pltpu.VMEM((2, tm, d), F32), pltpu.SemaphoreType.DMA],
        ),
        out_shape=jax.ShapeDtypeStruct((t, d), F32),
        compiler_params=_cparams("arbitrary"),
        name="moe_combine",
    )(pos, x, meta, yb)


def _moe(x, g, w_rt, b_rt, w_gate, w_up, w_down):
    t, d = x.shape
    meta, cnt = _router(x, g, w_rt, b_rt)
    bm = 256 if 2 * t >= 256 * N_EXPERTS * 4 else 128
    counts = cnt[0, :N_EXPERTS].astype(I32)
    padded = ((counts + bm - 1) // bm) * bm
    pend = jnp.cumsum(padded)
    pstart = pend - padded
    n_blocks = (2 * t + N_EXPERTS * (bm - 1) + bm - 1) // bm
    eid = meta[:, META_E:META_E + 2].astype(I32)
    rank = meta[:, META_R:META_R + 2].astype(I32)
    onehot = eid[:, :, None] == jnp.arange(N_EXPERTS, dtype=I32)[None, None, :]
    pos = (jnp.sum(jnp.where(onehot, pstart[None, None, :], 0), axis=-1) + rank).reshape(2 * t)
    blk_e = jnp.minimum(jnp.searchsorted(pend, jnp.arange(n_blocks, dtype=I32) * bm, side='right'),
                        N_EXPERTS - 1).astype(I32)
    n_used = (pend[-1:] // bm).astype(I32)
    xb = _dispatch(pos, x, g, n_blocks * bm)
    yb = _experts(blk_e, n_used, xb, w_gate, w_up, w_down, bm)
    return _combine(pos, x, meta, yb)


def _pad_lanes(a):
    return jnp.pad(a, [(0, 0)] * (a.ndim - 1) + [(0, LANES - a.shape[-1])])


def kernel(x_prompt, x_sample, mem_prompt, cache_fox_k, cache_fox_v, cache_fox_logf, state_conv_a, state_conv_c, cache_mem_k, cache_mem_v, norm_mix, norm_xmem, norm_mem, norm_ffn, even_w_in, even_w_conv_a, fox_q_gain, fox_k_gain, fox_f_bias, even_w_out, odd_w_in, odd_w_conv_c, conf_ln_gain, conf_ln_bias, sgu_v_gain, sgu_w, sgu_b, odd_w_out, xmem_wq, xmem_wk, xmem_wv, xmem_q_gain, xmem_k_gain, xmem_wo, moe_w_group, moe_b_group, moe_w_router, moe_b_router, moe_w_gate, moe_w_up, moe_w_down):
    bp, sp, d = x_prompt.shape
    bs, ss, _ = x_sample.shape
    depth = norm_mix.shape[0]
    n_mem = mem_prompt.shape[1]
    d_a = even_w_conv_a.shape[2]
    conv_a = even_w_conv_a.shape[1]
    conv_c = odd_w_conv_c.shape[1]
    d_c = odd_w_conv_c.shape[2]
    h_b = fox_f_bias.shape[1]
    d_b = h_b * DH
    d_x = xmem_wq.shape[2]
    h_mem = d_x // DH
    n_main = 3 * d_a + 3 * d_b

    xp = x_prompt.reshape(bp * sp, d)
    xs = x_sample.reshape(bs * ss, d)
    mem2 = mem_prompt.reshape(bp * n_mem, d)
    outs = {k: [] for k in ('fk_p', 'fv_p', 'fl_p', 'fk_s', 'fv_s', 'fl_s', 'ca_p', 'ca_s', 'cc_p', 'cc_s', 'cv_s', 'mk_p', 'mv_p')}

    for i in range(depth):
        g_mix = norm_mix[i][None]
        if i % 2 == 0:
            e = i // 2
            w_in = even_w_in[e]
            w_main = w_in[:, :n_main].astype(BF16)
            w_f = _pad_lanes(w_in[:, n_main:]).astype(BF16)
            f_bias = _pad_lanes(fox_f_bias[e][None])
            w_out = even_w_out[e].astype(BF16)
            args = (g_mix, w_main, w_f, f_bias, even_w_conv_a[e], fox_q_gain[e][None], fox_k_gain[e][None])

            ya, q, k, kb, v, vb, lf, st = _even_in(xp, *args, jnp.zeros((bp, conv_a - 1, d_a), F32), sp)
            lf3 = lf.reshape(bp, sp, LANES)
            c = _cumsum_time(lf3)
            c_rows = jnp.swapaxes(c[:, :, :h_b], 1, 2)[:, :, None, :]
            yb = _fox_prompt(q.reshape(bp, sp, d_b), kb.reshape(bp, sp, d_b), vb.reshape(bp, sp, d_b), c, c_rows)
            xp = _out_proj(ya, yb.reshape(bp * sp, d_b), w_out, xp)
            outs['fk_p'].append(k.reshape(bp, sp, h_b, DH))
            outs['fv_p'].append(v.reshape(bp, sp, h_b, DH))
            outs['fl_p'].append(lf3[:, :, :h_b])
            outs['ca_p'].append(st)

            ya, q, k, kb, v, vb, lf, st = _even_in(xs, *args, state_conv_a[e], ss)
            lf3 = lf.reshape(bs, ss, LANES)
            past = cache_fox_logf.shape[2]
            lf_all = jnp.concatenate([_pad_lanes(cache_fox_logf[e]), lf3], axis=1)
            c = _cumsum_time(lf_all)
            c_rows = jnp.swapaxes(c[:, :, :h_b], 1, 2)[:, :, None, :]
            yb = _fox_sample(q.reshape(bs, ss, d_b), cache_fox_k[e].reshape(bs, past, d_b),
                             cache_fox_v[e].reshape(bs, past, d_b), kb.reshape(bs, ss, d_b),
                             vb.reshape(bs, ss, d_b), c, c_rows)
            xs = _out_proj(ya, yb.reshape(bs * ss, d_b), w_out, xs)
            outs['fk_s'].append(k.reshape(bs, ss, h_b, DH))
            outs['fv_s'].append(v.reshape(bs, ss, h_b, DH))
            outs['fl_s'].append(lf3[:, :, :h_b])
            outs['ca_s'].append(st)
        else:
            o = i // 2
            w_in = odd_w_in[o].astype(BF16)
            w_out = odd_w_out[o].astype(BF16)
            tril = jnp.tril(jnp.ones(sgu_w.shape[2:], dtype=bool))
            w_s = jnp.where(tril[None], sgu_w[o], 0).astype(BF16)
            b_s = sgu_b[o][:, :, None]

            def odd(x, prev, seq_len):
                lc = min(CHUNK_MLP, seq_len)
                return _odd_in(x, g_mix, w_in, odd_w_conv_c[o], conf_ln_gain[o][None], conf_ln_bias[o][None],
                               sgu_v_gain[o][None], w_s[:, :lc, :lc], b_s[:, :lc], prev, seq_len)

            yc, yd, _, st = odd(xp, jnp.zeros((bp, conv_c - 1, d_c), F32), sp)
            xp = _out_proj(yc, yd, w_out, xp)
            outs['cc_p'].append(st)
            yc, yd, vn, st = odd(xs, state_conv_c[o], ss)
            xs = _out_proj(yc, yd, w_out, xs)
            outs['cc_s'].append(st)
            outs['cv_s'].append(vn.reshape(bs, ss, -1))

        g_x = norm_xmem[i][None]
        wq = xmem_wq[i].astype(BF16)
        wo = xmem_wo[i].astype(BF16)
        q_gain = xmem_q_gain[i][None]
        mk, mv = _mem_kv(mem2, norm_mem[i][None], xmem_wk[i].astype(BF16), xmem_wv[i].astype(BF16), xmem_k_gain[i][None])
        mk3 = mk.reshape(bp, n_mem, d_x)
        mv3 = mv.reshape(bp, n_mem, d_x)
        outs['mk_p'].append(mk3.reshape(bp, n_mem, h_mem, DH))
        outs['mv_p'].append(mv3.reshape(bp, n_mem, h_mem, DH))
        xp = _mem_attn(xp, g_x, wq, q_gain, mk3, mv3, wo, sp)
        xs = _mem_attn(xs, g_x, wq, q_gain, cache_mem_k[i].reshape(bs, n_mem, d_x),
                       cache_mem_v[i].reshape(bs, n_mem, d_x), wo, ss)

        g_f = norm_ffn[i][None]
        w_rt = _pad_lanes(jnp.concatenate([moe_w_router[i], moe_w_group[i]], axis=1)).astype(BF16)
        b_rt = _pad_lanes(jnp.concatenate([moe_b_router[i], moe_b_group[i]])[None])
        xp = _moe(xp, g_f, w_rt, b_rt, moe_w_gate[i], moe_w_up[i], moe_w_down[i])
        xs = _moe(xs, g_f, w_rt, b_rt, moe_w_gate[i], moe_w_up[i], moe_w_down[i])

    st = lambda key: jnp.stack(outs[key])
    return (xp.reshape(bp, sp, d), xs.reshape(bs, ss, d),
            st('fk_p'), st('fv_p'), st('fl_p'),
            st('fk_s'), st('fv_s'), st('fl_s'),
            st('ca_p'), st('ca_s'),
            st('cc_p'), st('cc_s'),
            st('cv_s'),
            st('mk_p'), st('mv_p'))
```

```python
import functools

import jax
import jax.numpy as jnp
from jax import lax
from jax.experimental import pallas as pl
from jax.experimental.pallas import tpu as pltpu

F32 = jnp.float32
BF16 = jnp.bfloat16
I32 = jnp.int32

EPS = 1e-6
DH = 128
LANES = 128
LOG2E = 1.4426950408889634
CHUNK_MLP = 128
N_GROUPS = 4
EXP_PER_GROUP = 8
N_EXPERTS = N_GROUPS * EXP_PER_GROUP
V7X_VMEM_LIMIT = 56 * 1024 * 1024
ROW_TILE = 512
ATT_TILE = 512
CACHE_TILE = 512
CONV_ROWS, CONV_COLS = 32, 256
GATHER_TILE = 256


def _cparams(*sem):
    return pltpu.CompilerParams(dimension_semantics=sem, vmem_limit_bytes=V7X_VMEM_LIMIT)


def _rms(x, g):
    return x * lax.rsqrt(jnp.mean(x * x, axis=-1, keepdims=True) + EPS) * g


def _row_tile(n_rows, seq_len, cap=ROW_TILE):
    if seq_len >= cap:
        assert seq_len % cap == 0
        return cap
    nb = max(1, cap // seq_len)
    n_seq = n_rows // seq_len
    while n_seq % nb:
        nb -= 1
    return nb * seq_len


def _even_in_kernel(x_ref, g_ref, w_ref, wf_ref, fb_ref, wc_ref, qg_ref, kg_ref, prev_ref,
                    ya_ref, q_ref, k_ref, kb_ref, v_ref, vb_ref, lf_ref, st_ref,
                    xn_s, ab_s, ac_s, ext_s, *, nb, lt, tiles_per_seq, n_heads):
    i = pl.program_id(0)
    j = pl.program_id(1)
    halo = 8

    @pl.when(j == 0)
    def _():
        xn = _rms(x_ref[...], g_ref[...]).astype(BF16)
        xn_s[...] = xn
        z = jnp.dot(xn, wf_ref[...], preferred_element_type=F32) + fb_ref[...]
        lf_ref[...] = jnp.minimum(z, 0.0) - jnp.log1p(jnp.exp(-jnp.abs(z)))

    acc = jnp.dot(xn_s[...], w_ref[...], preferred_element_type=F32)

    @pl.when(j == 0)
    def _():
        ab_s[...] = acc

    @pl.when(j == 1)
    def _():
        ac_s[...] = acc

    @pl.when(j == 2)
    def _():
        gated = ac_s[...] * acc
        for n in range(nb):
            ext_s[n, halo:halo + lt, :] = gated[n * lt:(n + 1) * lt, :]

        @pl.when(i % tiles_per_seq == 0)
        def _():
            for n in range(nb):
                ext_s[n, halo - 2:halo, :] = prev_ref[n]

        w = wc_ref[...]
        for n in range(nb):
            conv = (w[0:1, :] * ext_s[n, halo - 2:halo - 2 + lt, :]
                    + w[1:2, :] * ext_s[n, halo - 1:halo - 1 + lt, :]
                    + w[2:3, :] * ext_s[n, halo:halo + lt, :])
            ya_ref[n * lt:(n + 1) * lt, :] = (ab_s[n * lt:(n + 1) * lt, :] * conv).astype(BF16)
            tail = ext_s[n, halo - 2 + lt:halo + lt, :]
            st_ref[n] = tail
            ext_s[n, halo - 2:halo, :] = tail

    def head_norm(gain):
        for h in range(n_heads):
            seg = acc[:, h * DH:(h + 1) * DH]
            yield h, seg * lax.rsqrt(jnp.mean(seg * seg, axis=-1, keepdims=True) + EPS) * gain

    @pl.when(j == 3)
    def _():
        for h, qn in head_norm(qg_ref[...]):
            q_ref[:, h * DH:(h + 1) * DH] = (qn * (DH ** -0.5 * LOG2E)).astype(BF16)

    @pl.when(j == 4)
    def _():
        for h, kn in head_norm(kg_ref[...]):
            k_ref[:, h * DH:(h + 1) * DH] = kn
            kb_ref[:, h * DH:(h + 1) * DH] = kn.astype(BF16)

    @pl.when(j == 5)
    def _():
        v_ref[...] = acc
        vb_ref[...] = acc.astype(BF16)


def _even_in(x, g, w_main, w_f, f_bias, w_conv, q_gain, k_gain, prev, seq_len):
    t, d = x.shape
    dc = w_conv.shape[1]
    n_col = w_main.shape[1] // dc
    assert n_col == 6
    n_heads = dc // DH
    tm = _row_tile(t, seq_len)
    lt = min(seq_len, tm)
    nb = tm // lt
    tiles_per_seq = seq_len // lt
    n_seq = t // seq_len
    kern = functools.partial(_even_in_kernel, nb=nb, lt=lt, tiles_per_seq=tiles_per_seq, n_heads=n_heads)
    row = lambda i, j: (i, 0)
    const = lambda i, j: (0, 0)
    seq3 = lambda i, j: (i // tiles_per_seq, 0, 0)
    return pl.pallas_call(
        kern,
        grid=(t // tm, n_col),
        in_specs=[
            pl.BlockSpec((tm, d), row),
            pl.BlockSpec((1, d), const),
            pl.BlockSpec((d, dc), lambda i, j: (0, j)),
            pl.BlockSpec((d, LANES), const),
            pl.BlockSpec((1, LANES), const),
            pl.BlockSpec(w_conv.shape, const),
            pl.BlockSpec((1, DH), const),
            pl.BlockSpec((1, DH), const),
            pl.BlockSpec((nb, 2, dc), seq3),
        ],
        out_specs=[
            pl.BlockSpec((tm, dc), row),
            pl.BlockSpec((tm, dc), row),
            pl.BlockSpec((tm, dc), row),
            pl.BlockSpec((tm, dc), row),
            pl.BlockSpec((tm, dc), row),
            pl.BlockSpec((tm, dc), row),
            pl.BlockSpec((tm, LANES), row),
            pl.BlockSpec((nb, 2, dc), seq3),
        ],
        out_shape=[
            jax.ShapeDtypeStruct((t, dc), BF16),
            jax.ShapeDtypeStruct((t, dc), BF16),
            jax.ShapeDtypeStruct((t, dc), F32),
            jax.ShapeDtypeStruct((t, dc), BF16),
            jax.ShapeDtypeStruct((t, dc), F32),
            jax.ShapeDtypeStruct((t, dc), BF16),
            jax.ShapeDtypeStruct((t, LANES), F32),
            jax.ShapeDtypeStruct((n_seq, 2, dc), F32),
        ],
        scratch_shapes=[
            pltpu.VMEM((tm, d), BF16),
            pltpu.VMEM((tm, dc), F32),
            pltpu.VMEM((tm, dc), F32),
            pltpu.VMEM((nb, 8 + lt, dc), F32),
        ],
        compiler_params=_cparams("arbitrary", "arbitrary"),
        name="even_in",
    )(x, g, w_main, w_f, f_bias, w_conv, q_gain, k_gain, prev)


def _cumsum_kernel(lf_ref, c_ref, carry_s, *, ts):
    @pl.when(pl.program_id(1) == 0)
    def _():
        carry_s[...] = jnp.zeros_like(carry_s)

    x = lf_ref[0]
    row = lax.broadcasted_iota(I32, x.shape, 0)
    d = 1
    while d < ts:
        x = x + jnp.where(row >= d, pltpu.roll(x, d, axis=0), 0.0)
        d *= 2
    x = x + carry_s[...]
    c_ref[0] = x * LOG2E
    carry_s[...] = x[ts - 1:ts, :]


def _time_tile(s, cap=1024):
    ts = min(s, cap)
    while s % ts or ts % 8:
        ts -= 1
    return ts


def _cumsum_time(lf):
    b, s, _ = lf.shape
    ts = _time_tile(s)
    return pl.pallas_call(
        functools.partial(_cumsum_kernel, ts=ts),
        grid=(b, s // ts),
        in_specs=[pl.BlockSpec((1, ts, LANES), lambda bi, si: (bi, si, 0))],
        out_specs=pl.BlockSpec((1, ts, LANES), lambda bi, si: (bi, si, 0)),
        out_shape=jax.ShapeDtypeStruct(lf.shape, F32),
        scratch_shapes=[pltpu.VMEM((1, LANES), F32)],
        compiler_params=_cparams("arbitrary", "arbitrary"),
        name="cumsum_time",
    )(lf)


def _online_softmax_step(s, v, m, l, acc):
    m_new = jnp.maximum(m, jnp.max(s, axis=1, keepdims=True))
    p = jnp.exp2(s - m_new)
    alpha = jnp.exp2(m - m_new)
    l = alpha * l + jnp.sum(p, axis=1, keepdims=True)
    acc = alpha * acc + jnp.dot(p.astype(BF16), v, preferred_element_type=F32)
    return m_new, l, acc


def _fox_prompt_kernel(q_ref, k_ref, v_ref, cq_ref, ck_ref, o_ref, *, tq):
    h = pl.program_id(1)
    qi = pl.program_id(2)
    q = q_ref[0]
    lane = lax.broadcasted_iota(I32, (tq, LANES), 1)
    cq = jnp.sum(jnp.where(lane == h, cq_ref[0], 0.0), axis=1, keepdims=True)

    def scores(kj):
        ks = pl.multiple_of(kj * tq, tq)
        k = k_ref[0, pl.ds(ks, tq), :]
        v = v_ref[0, pl.ds(ks, tq), :]
        s = lax.dot_general(q, k, (((1,), (1,)), ((), ())), preferred_element_type=F32)
        return s + (cq - ck_ref[0, 0, :, pl.ds(ks, tq)]), v

    def past_block(kj, carry):
        s, v = scores(kj)
        return _online_softmax_step(s, v, *carry)

    init = (jnp.full((tq, 1), -jnp.inf, F32), jnp.zeros((tq, 1), F32), jnp.zeros((tq, DH), F32))
    m, l, acc = lax.fori_loop(0, qi, past_block, init)
    s, v = scores(qi)
    r = lax.broadcasted_iota(I32, (tq, tq), 0)
    c = lax.broadcasted_iota(I32, (tq, tq), 1)
    s = jnp.where(c <= r, s, -jnp.inf)
    m, l, acc = _online_softmax_step(s, v, m, l, acc)
    o_ref[0] = (acc / l).astype(BF16)


def _fox_prompt(q, k, v, c_pad, c_rows):
    b, s, hd = q.shape
    n_heads = hd // DH
    tq = min(ATT_TILE, s)
    assert s % tq == 0
    return pl.pallas_call(
        functools.partial(_fox_prompt_kernel, tq=tq),
        grid=(b, n_heads, s // tq),
        in_specs=[
            pl.BlockSpec((1, tq, DH), lambda bi, h, qi: (bi, qi, h)),
            pl.BlockSpec((1, s, DH), lambda bi, h, qi: (bi, 0, h)),
            pl.BlockSpec((1, s, DH), lambda bi, h, qi: (bi, 0, h)),
            pl.BlockSpec((1, tq, LANES), lambda bi, h, qi: (bi, qi, 0)),
            pl.BlockSpec((1, 1, 1, s), lambda bi, h, qi: (bi, h, 0, 0)),
        ],
        out_specs=pl.BlockSpec((1, tq, DH), lambda bi, h, qi: (bi, qi, h)),
        out_shape=jax.ShapeDtypeStruct((b, s, hd), BF16),
        compiler_params=_cparams("arbitrary", "arbitrary", "arbitrary"),
        name="fox_prompt",
    )(q, k, v, c_pad, c_rows)


def _fox_sample_kernel(q_ref, ck_ref, cv_ref, kn_ref, vn_ref, cq_ref, cr_ref, o_ref,
                       m_s, l_s, acc_s, *, n_heads, t, tk, past):
    kj = pl.program_id(1)
    last = pl.num_programs(1) - 1

    @pl.when(kj == 0)
    def _():
        m_s[...] = jnp.full_like(m_s, -jnp.inf)
        l_s[...] = jnp.zeros_like(l_s)
        acc_s[...] = jnp.zeros_like(acc_s)

    lane = lax.broadcasted_iota(I32, (t, LANES), 1)
    ks = pl.multiple_of(kj * tk, tk)

    def head_q(h):
        cq = jnp.sum(jnp.where(lane == h, cq_ref[0], 0.0), axis=1, keepdims=True)
        return q_ref[0, :, h * DH:(h + 1) * DH], cq

    for h in range(n_heads):
        q, cq = head_q(h)
        k = ck_ref[pl.ds(h, tk, stride=n_heads), :].astype(BF16)
        v = cv_ref[pl.ds(h, tk, stride=n_heads), :].astype(BF16)
        s = lax.dot_general(q, k, (((1,), (1,)), ((), ())), preferred_element_type=F32)
        s = s + (cq - cr_ref[0, h, :, pl.ds(ks, tk)])
        m_s[h], l_s[h], acc_s[h] = _online_softmax_step(s, v, m_s[h], l_s[h], acc_s[h])

    @pl.when(kj == last)
    def _():
        r = lax.broadcasted_iota(I32, (t, t), 0)
        c = lax.broadcasted_iota(I32, (t, t), 1)
        for h in range(n_heads):
            sl = slice(h * DH, (h + 1) * DH)
            q, cq = head_q(h)
            s = lax.dot_general(q, kn_ref[0, :, sl], (((1,), (1,)), ((), ())), preferred_element_type=F32)
            s = s + (cq - cr_ref[0, h, :, past:past + t])
            s = jnp.where(c <= r, s, -jnp.inf)
            _, l, acc = _online_softmax_step(s, vn_ref[0, :, sl], m_s[h], l_s[h], acc_s[h])
            o_ref[0, :, sl] = (acc / l).astype(BF16)


def _fox_sample(q, cache_k, cache_v, layer, k_new, v_new, c_pad, c_rows):
    b, t, hd = q.shape
    n_heads = hd // DH
    past = cache_k.shape[2]
    tk = min(CACHE_TILE, past)
    assert past % tk == 0 and past % t == 0 and tk % LANES == 0
    assert cache_k.shape[1:] == (b, past, n_heads, DH)
    kern = functools.partial(_fox_sample_kernel, n_heads=n_heads, t=t, tk=tk, past=past)
    per_b = lambda bi, kj: (bi, 0, 0)
    n_kb = past // tk
    cache_rows = lambda bi, kj: ((layer * b + bi) * n_kb + kj, 0)
    return pl.pallas_call(
        kern,
        grid=(b, n_kb),
        in_specs=[
            pl.BlockSpec((1, t, hd), per_b),
            pl.BlockSpec((tk * n_heads, DH), cache_rows),
            pl.BlockSpec((tk * n_heads, DH), cache_rows),
            pl.BlockSpec((1, t, hd), per_b),
            pl.BlockSpec((1, t, hd), per_b),
            pl.BlockSpec((1, t, LANES), lambda bi, kj: (bi, past // t, 0)),
            pl.BlockSpec((1, n_heads, 1, past + t), lambda bi, kj: (bi, 0, 0, 0)),
        ],
        out_specs=pl.BlockSpec((1, t, hd), per_b),
        out_shape=jax.ShapeDtypeStruct((b, t, hd), BF16),
        scratch_shapes=[
            pltpu.VMEM((n_heads, t, 1), F32),
            pltpu.VMEM((n_heads, t, 1), F32),
            pltpu.VMEM((n_heads, t, DH), F32),
        ],
        compiler_params=_cparams("arbitrary", "arbitrary"),
        name="fox_sample",
    )(q, cache_k.reshape(-1, DH), cache_v.reshape(-1, DH), k_new, v_new, c_pad, c_rows)


def _out_proj_kernel(a_ref, b_ref, wa_ref, wb_ref, r_ref, o_ref):
    o_ref[...] = (r_ref[...]
                  + jnp.dot(a_ref[...], wa_ref[...], preferred_element_type=F32)
                  + jnp.dot(b_ref[...], wb_ref[...], preferred_element_type=F32))


def _out_proj(a, b, w, res):
    t, ka = a.shape
    kb = b.shape[1]
    d = w.shape[1]
    tm = _row_tile(t, t)
    assert ka == kb
    return pl.pallas_call(
        _out_proj_kernel,
        grid=(t // tm,),
        in_specs=[
            pl.BlockSpec((tm, ka), lambda i: (i, 0)),
            pl.BlockSpec((tm, kb), lambda i: (i, 0)),
            pl.BlockSpec((ka, d), lambda i: (0, 0)),
            pl.BlockSpec((kb, d), lambda i: (1, 0)),
            pl.BlockSpec((tm, d), lambda i: (i, 0)),
        ],
        out_specs=pl.BlockSpec((tm, d), lambda i: (i, 0)),
        out_shape=jax.ShapeDtypeStruct((t, d), F32),
        compiler_params=_cparams("arbitrary"),
        name="out_proj",
    )(a, b, w, w, res)


def _odd_in_kernel(x_ref, g_ref, w_ref, wc_ref, lg_ref, lb_ref, vg_ref, ws_ref, bs_ref, prev_ref,
                   yc_ref, yd_ref, vn_ref, st_ref,
                   xn_s, val_s, u_s, ext_s, conv_s, *, nb, lt, tiles_per_seq, taps, n_groups, lc):
    i = pl.program_id(0)
    j = pl.program_id(1)
    hist = taps - 1
    halo = ((hist + 7) // 8) * 8

    @pl.when(j == 0)
    def _():
        xn_s[...] = _rms(x_ref[...], g_ref[...]).astype(BF16)

    acc = jnp.dot(xn_s[...], w_ref[...], preferred_element_type=F32)

    @pl.when(j == 0)
    def _():
        val_s[...] = acc

    @pl.when(j == 1)
    def _():
        glu = val_s[...] * jax.nn.sigmoid(acc)
        for n in range(nb):
            ext_s[n, halo:halo + lt, :] = glu[n * lt:(n + 1) * lt, :]

        @pl.when(i % tiles_per_seq == 0)
        def _():
            for n in range(nb):
                ext_s[n, halo - hist:halo, :] = prev_ref[n]
                if halo > hist:
                    ext_s[n, 0:halo - hist, :] = jnp.zeros((halo - hist, ext_s.shape[2]), F32)

        dc = conv_s.shape[1]
        off0 = halo - hist
        win = ((off0 + hist + CONV_ROWS + 7) // 8) * 8
        for n in range(nb):
            for r0 in range(0, lt, CONV_ROWS):
                for c0 in range(0, dc, CONV_COLS):
                    cols = slice(c0, c0 + CONV_COLS)
                    blk = ext_s[n, r0:r0 + win, cols]
                    part = None
                    for b in range(8):
                        rolled = blk if b == 0 else pltpu.roll(blk, win - b, axis=0)
                        for a in range(win // 8):
                            s = 8 * a + b - off0
                            if 0 <= s < taps and 8 * a + CONV_ROWS <= win:
                                term = wc_ref[s:s + 1, cols] * rolled[8 * a:8 * a + CONV_ROWS, :]
                                part = term if part is None else part + term
                    conv_s[n * lt + r0:n * lt + r0 + CONV_ROWS, cols] = part
        for n in range(nb):
            conv = conv_s[n * lt:(n + 1) * lt, :]
            mu = jnp.mean(conv, axis=-1, keepdims=True)
            xc = conv - mu
            var = jnp.mean(xc * xc, axis=-1, keepdims=True)
            y = xc * lax.rsqrt(var + EPS) * lg_ref[...] + lb_ref[...]
            yc_ref[n * lt:(n + 1) * lt, :] = (y * jax.nn.sigmoid(y)).astype(BF16)
            tail = ext_s[n, halo - hist + lt:halo + lt, :]
            st_ref[n] = tail
            ext_s[n, halo - hist:halo, :] = tail

    @pl.when(j == 2)
    def _():
        u_s[...] = acc

    @pl.when(j == 3)
    def _():
        vn = _rms(acc, vg_ref[...])
        vn_ref[...] = vn
        vb = vn.astype(BF16)
        gw = vn.shape[1] // n_groups
        for c in range(vn.shape[0] // lc):
            rows = slice(c * lc, (c + 1) * lc)
            for g in range(n_groups):
                cols = slice(g * gw, (g + 1) * gw)
                gate = jnp.dot(ws_ref[g], vb[rows, cols], preferred_element_type=F32) + bs_ref[g]
                yd_ref[rows, cols] = (u_s[rows, cols] * gate).astype(BF16)


def _odd_in(x, g, w_in, w_conv, ln_g, ln_b, v_gain, w_s, b_s, prev, seq_len):
    t, d = x.shape
    taps, dc = w_conv.shape
    assert w_in.shape[1] == 4 * dc
    n_groups = w_s.shape[0]
    tm = _row_tile(t, seq_len, cap=ROW_TILE if seq_len >= ROW_TILE else ROW_TILE // 2)
    lt = min(seq_len, tm)
    nb = tm // lt
    tiles_per_seq = seq_len // lt
    n_seq = t // seq_len
    lc = min(CHUNK_MLP, seq_len)
    assert lt % lc == 0 and lt % CONV_ROWS == 0 and dc % CONV_COLS == 0
    hist = taps - 1
    halo = ((hist + 7) // 8) * 8
    kern = functools.partial(_odd_in_kernel, nb=nb, lt=lt, tiles_per_seq=tiles_per_seq, taps=taps,
                             n_groups=n_groups, lc=lc)
    row = lambda i, j: (i, 0)
    const = lambda i, j: (0, 0)
    const3 = lambda i, j: (0, 0, 0)
    seq3 = lambda i, j: (i // tiles_per_seq, 0, 0)
    return pl.pallas_call(
        kern,
        grid=(t // tm, 4),
        in_specs=[
            pl.BlockSpec((tm, d), row),
            pl.BlockSpec((1, d), const),
            pl.BlockSpec((d, dc), lambda i, j: (0, j)),
            pl.BlockSpec((taps, dc), const),
            pl.BlockSpec((1, dc), const),
            pl.BlockSpec((1, dc), const),
            pl.BlockSpec((1, dc), const),
            pl.BlockSpec((n_groups, lc, lc), const3),
            pl.BlockSpec((n_groups, lc, 1), const3),
            pl.BlockSpec((nb, hist, dc), seq3),
        ],
        out_specs=[
            pl.BlockSpec((tm, dc), row),
            pl.BlockSpec((tm, dc), row),
            pl.BlockSpec((tm, dc), row),
            pl.BlockSpec((nb, hist, dc), seq3),
        ],
        out_shape=[
            jax.ShapeDtypeStruct((t, dc), BF16),
            jax.ShapeDtypeStruct((t, dc), BF16),
            jax.ShapeDtypeStruct((t, dc), F32),
            jax.ShapeDtypeStruct((n_seq, hist, dc), F32),
        ],
        scratch_shapes=[
            pltpu.VMEM((tm, d), BF16),
            pltpu.VMEM((tm, dc), F32),
            pltpu.VMEM((tm, dc), F32),
            pltpu.VMEM((nb, halo + lt, dc), F32),
            pltpu.VMEM((tm, dc), F32),
        ],
        compiler_params=_cparams("arbitrary", "arbitrary"),
        name="odd_in",
    )(x, g, w_in, w_conv, ln_g, ln_b, v_gain, w_s, b_s, prev)


def _mem_kv_kernel(m_ref, g_ref, wk_ref, wv_ref, kg_ref, k_ref, v_ref, *, n_heads):
    mn = _rms(m_ref[...], g_ref[...]).astype(BF16)
    k = jnp.dot(mn, wk_ref[...], preferred_element_type=F32)
    for h in range(n_heads):
        seg = k[:, h * DH:(h + 1) * DH]
        k_ref[:, h * DH:(h + 1) * DH] = seg * lax.rsqrt(jnp.mean(seg * seg, axis=-1, keepdims=True) + EPS) * kg_ref[...]
    v_ref[...] = jnp.dot(mn, wv_ref[...], preferred_element_type=F32)


def _mem_kv(mem, g, wk, wv, k_gain):
    t, d = mem.shape
    dx = wk.shape[1]
    tm = _row_tile(t, t)
    return pl.pallas_call(
        functools.partial(_mem_kv_kernel, n_heads=dx // DH),
        grid=(t // tm,),
        in_specs=[
            pl.BlockSpec((tm, d), lambda i: (i, 0)),
            pl.BlockSpec((1, d), lambda i: (0, 0)),
            pl.BlockSpec((d, dx), lambda i: (0, 0)),
            pl.BlockSpec((d, dx), lambda i: (0, 0)),
            pl.BlockSpec((1, DH), lambda i: (0, 0)),
        ],
        out_specs=[pl.BlockSpec((tm, dx), lambda i: (i, 0))] * 2,
        out_shape=[jax.ShapeDtypeStruct((t, dx), F32)] * 2,
        compiler_params=_cparams("arbitrary"),
        name="mem_kv",
    )(mem, g, wk, wv, k_gain)


def _mem_attn_kernel(x_ref, g_ref, wq_ref, qg_ref, k_ref, v_ref, wo_ref, o_ref, att_s, *, nb, lt, n_heads):
    x = x_ref[...]
    xn = _rms(x, g_ref[...]).astype(BF16)
    q = jnp.dot(xn, wq_ref[...], preferred_element_type=F32)
    for h in range(n_heads):
        sl = slice(h * DH, (h + 1) * DH)
        seg = q[:, sl]
        qn = seg * lax.rsqrt(jnp.mean(seg * seg, axis=-1, keepdims=True) + EPS) * qg_ref[...]
        qb = (qn * (DH ** -0.5)).astype(BF16)
        for n in range(nb):
            rows = slice(n * lt, (n + 1) * lt)
            k = k_ref[n, :, sl].astype(BF16)
            v = v_ref[n, :, sl].astype(BF16)
            s = lax.dot_general(qb[rows], k, (((1,), (1,)), ((), ())), preferred_element_type=F32)
            p = jnp.exp(s - jnp.max(s, axis=1, keepdims=True))
            o = jnp.dot(p.astype(BF16), v, preferred_element_type=F32) / jnp.sum(p, axis=1, keepdims=True)
            att_s[rows, sl] = o.astype(BF16)
    o_ref[...] = x + jnp.dot(att_s[...], wo_ref[...], preferred_element_type=F32)


def _mem_attn(x, g, wq, q_gain, k, v, wo, seq_len):
    t, d = x.shape
    dx = wq.shape[1]
    n_mem = k.shape[1]
    tm = _row_tile(t, seq_len, cap=256 if seq_len < 256 else ROW_TILE)
    lt = min(seq_len, tm)
    nb = tm // lt
    tiles_per_seq = seq_len // lt
    kern = functools.partial(_mem_attn_kernel, nb=nb, lt=lt, n_heads=dx // DH)
    return pl.pallas_call(
        kern,
        grid=(t // tm,),
        in_specs=[
            pl.BlockSpec((tm, d), lambda i: (i, 0)),
            pl.BlockSpec((1, d), lambda i: (0, 0)),
            pl.BlockSpec((d, dx), lambda i: (0, 0)),
            pl.BlockSpec((1, DH), lambda i: (0, 0)),
            pl.BlockSpec((nb, n_mem, dx), lambda i: (i // tiles_per_seq, 0, 0)),
            pl.BlockSpec((nb, n_mem, dx), lambda i: (i // tiles_per_seq, 0, 0)),
            pl.BlockSpec((dx, d), lambda i: (0, 0)),
        ],
        out_specs=pl.BlockSpec((tm, d), lambda i: (i, 0)),
        out_shape=jax.ShapeDtypeStruct((t, d), F32),
        scratch_shapes=[pltpu.VMEM((tm, dx), BF16)],
        compiler_params=_cparams("arbitrary"),
        name="mem_attn",
    )(x, g, wq, q_gain, k, v, wo)


GROUP_LANE0 = N_EXPERTS
META_E, META_G, META_R = 0, 2, 4


def _router_kernel(x_ref, g_ref, w_ref, b_ref, meta_ref, cnt_ref, *, tm):
    @pl.when(pl.program_id(0) == 0)
    def _():
        cnt_ref[...] = jnp.zeros_like(cnt_ref)

    xn = _rms(x_ref[...], g_ref[...]).astype(BF16)
    logits = jnp.dot(xn, w_ref[...], preferred_element_type=F32) + b_ref[...]
    lane = lax.broadcasted_iota(I32, (tm, LANES), 1)
    big = jnp.int32(LANES)

    def first_argmax(vals):
        top = jnp.max(vals, axis=1, keepdims=True)
        return top, jnp.min(jnp.where(vals == top, lane, big), axis=1, keepdims=True)

    is_grp = (lane >= GROUP_LANE0) & (lane < GROUP_LANE0 + N_GROUPS)
    lg = jnp.where(is_grp, logits, -jnp.inf)
    g_top, g_lane = first_argmax(lg)
    p_sel = 1.0 / jnp.sum(jnp.where(is_grp, jnp.exp(lg - g_top), 0.0), axis=1, keepdims=True)
    e0 = (g_lane - GROUP_LANE0) * EXP_PER_GROUP
    le = jnp.where((lane >= e0) & (lane < e0 + EXP_PER_GROUP), logits, -jnp.inf)
    v1, i1 = first_argmax(le)
    v2, i2 = first_argmax(jnp.where(lane == i1, -jnp.inf, le))
    e = jnp.exp(v2 - v1)
    g1 = p_sel / (1.0 + e)
    g2 = p_sel * e / (1.0 + e)

    sel1 = lane == i1
    sel2 = lane == i2
    onehot = jnp.where(sel1 | sel2, 1.0, 0.0)
    r = lax.broadcasted_iota(I32, (tm, tm), 0)
    c = lax.broadcasted_iota(I32, (tm, tm), 1)
    earlier = jnp.where(c < r, 1.0, 0.0).astype(BF16)
    before = jnp.dot(earlier, onehot.astype(BF16), preferred_element_type=F32) + cnt_ref[...]
    r1 = jnp.sum(jnp.where(sel1, before, 0.0), axis=1, keepdims=True)
    r2 = jnp.sum(jnp.where(sel2, before, 0.0), axis=1, keepdims=True)
    cnt_ref[...] = cnt_ref[...] + jnp.sum(onehot, axis=0, keepdims=True)

    meta = jnp.zeros((tm, LANES), F32)
    for k, val in ((META_E, i1.astype(F32)), (META_E + 1, i2.astype(F32)), (META_G, g1), (META_G + 1, g2),
                   (META_R, r1), (META_R + 1, r2)):
        meta = jnp.where(lane == k, val, meta)
    meta_ref[...] = meta


def _router(x, g, w_rt, b_rt):
    t, d = x.shape
    tm = _row_tile(t, t)
    return pl.pallas_call(
        functools.partial(_router_kernel, tm=tm),
        grid=(t // tm,),
        in_specs=[
            pl.BlockSpec((tm, d), lambda i: (i, 0)),
            pl.BlockSpec((1, d), lambda i: (0, 0)),
            pl.BlockSpec((d, LANES), lambda i: (0, 0)),
            pl.BlockSpec((1, LANES), lambda i: (0, 0)),
        ],
        out_specs=[pl.BlockSpec((tm, LANES), lambda i: (i, 0)), pl.BlockSpec((1, LANES), lambda i: (0, 0))],
        out_shape=[jax.ShapeDtypeStruct((t, LANES), F32), jax.ShapeDtypeStruct((1, LANES), F32)],
        compiler_params=_cparams("arbitrary"),
        name="moe_router",
    )(x, g, w_rt, b_rt)


def _row_copy(src, src_row, dst, dst_row, sem):
    return pltpu.make_async_copy(src.at[pl.ds(src_row, 1)], dst.at[pl.ds(dst_row, 1)], sem)


def _dispatch_kernel(pos_ref, x_ref, g_ref, zero_ref, xb_ref, xn_s, sem, *, tm):
    del zero_ref
    i = pl.program_id(0)
    n_steps = pl.num_programs(0)
    slot = i % 2
    base = i * (2 * tm)

    def drain(sl):
        def wait_row(r, _):
            _row_copy(xn_s.at[sl], 0, xb_ref, 0, sem.at[sl]).wait()
            _row_copy(xn_s.at[sl], 0, xb_ref, 0, sem.at[sl]).wait()
            return 0

        lax.fori_loop(0, tm, wait_row, 0)

    @pl.when(i >= 2)
    def _():
        drain(slot)

    xn_s[slot] = _rms(x_ref[...], g_ref[...])

    def issue(r, _):
        _row_copy(xn_s.at[slot], r, xb_ref, pos_ref[base + 2 * r], sem.at[slot]).start()
        _row_copy(xn_s.at[slot], r, xb_ref, pos_ref[base + 2 * r + 1], sem.at[slot]).start()
        return 0

    lax.fori_loop(0, tm, issue, 0)

    @pl.when(i == n_steps - 1)
    def _():
        @pl.when(n_steps >= 2)
        def _():
            drain(1 - slot)

        drain(slot)


def _dispatch(pos, x, g, n_rows):
    t, d = x.shape
    tm = min(GATHER_TILE, t)
    assert t % tm == 0
    zeros = jnp.zeros((n_rows, d), F32)
    return pl.pallas_call(
        functools.partial(_dispatch_kernel, tm=tm),
        grid_spec=pltpu.PrefetchScalarGridSpec(
            num_scalar_prefetch=1,
            grid=(t // tm,),
            in_specs=[
                pl.BlockSpec((tm, d), lambda i, pos: (i, 0)),
                pl.BlockSpec((1, d), lambda i, pos: (0, 0)),
                pl.BlockSpec(memory_space=pl.ANY),
            ],
            out_specs=pl.BlockSpec(memory_space=pl.ANY),
            scratch_shapes=[pltpu.VMEM((2, tm, d), F32), pltpu.SemaphoreType.DMA((2,))],
        ),
        out_shape=jax.ShapeDtypeStruct((n_rows, d), F32),
        input_output_aliases={3: 0},
        compiler_params=_cparams("arbitrary"),
        name="moe_dispatch",
    )(pos, x, g, zeros)


def _expert_kernel(be_ref, nu_ref, x_ref, wg_ref, wu_ref, wd_ref, y_ref, wg_s, wu_s, wd_s):
    b = pl.program_id(0)
    used = b < nu_ref[0]
    prev = be_ref[jnp.maximum(b - 1, 0)]

    @pl.when(used & ((b == 0) | (be_ref[b] != prev)))
    def _():
        wg_s[...] = wg_ref[0, 0].astype(BF16)
        wu_s[...] = wu_ref[0, 0].astype(BF16)
        wd_s[...] = wd_ref[0, 0].astype(BF16)

    @pl.when(used)
    def _():
        x = x_ref[...].astype(BF16)
        gate = jnp.dot(x, wg_s[...], preferred_element_type=F32)
        up = jnp.dot(x, wu_s[...], preferred_element_type=F32)
        hid = (gate * jax.nn.sigmoid(gate) * up).astype(BF16)
        y_ref[...] = jnp.dot(hid, wd_s[...], preferred_element_type=F32)

    @pl.when(jnp.logical_not(used))
    def _():
        y_ref[...] = jnp.zeros_like(y_ref)


def _experts(blk_e, n_used, xb, w_gate, w_up, w_down, layer, bm):
    p, d = xb.shape
    de = w_gate.shape[3]
    n_blocks = p // bm

    def xmap(b, be, nu):
        return (jnp.minimum(b, jnp.maximum(nu[0] - 1, 0)), 0)

    def wmap(b, be, nu):
        return (layer, be[b], 0, 0)

    return pl.pallas_call(
        _expert_kernel,
        grid_spec=pltpu.PrefetchScalarGridSpec(
            num_scalar_prefetch=2,
            grid=(n_blocks,),
            in_specs=[
                pl.BlockSpec((bm, d), xmap),
                pl.BlockSpec((1, 1, d, de), wmap),
                pl.BlockSpec((1, 1, d, de), wmap),
                pl.BlockSpec((1, 1, de, d), wmap),
            ],
            out_specs=pl.BlockSpec((bm, d), lambda b, be, nu: (b, 0)),
            scratch_shapes=[pltpu.VMEM((d, de), BF16), pltpu.VMEM((d, de), BF16), pltpu.VMEM((de, d), BF16)],
        ),
        out_shape=jax.ShapeDtypeStruct((p, d), F32),
        compiler_params=_cparams("arbitrary"),
        name="moe_experts",
    )(blk_e, n_used, xb, w_gate, w_up, w_down)


def _combine_kernel(pos_ref, x_ref, meta_ref, yb_ref, o_ref, buf_s, sem, *, tm):
    i = pl.program_id(0)
    slot = i % 2

    def issue(tile, sl):
        base = tile * (2 * tm)

        def start_row(r, _):
            _row_copy(yb_ref, pos_ref[base + 2 * r], buf_s.at[sl, 0], r, sem.at[sl]).start()
            _row_copy(yb_ref, pos_ref[base + 2 * r + 1], buf_s.at[sl, 1], r, sem.at[sl]).start()
            return 0

        lax.fori_loop(0, tm, start_row, 0)

    @pl.when(i == 0)
    def _():
        issue(0, 0)

    @pl.when(i + 1 < pl.num_programs(0))
    def _():
        issue(i + 1, 1 - slot)

    def wait_row(r, _):
        _row_copy(yb_ref, 0, buf_s.at[slot, 0], 0, sem.at[slot]).wait()
        _row_copy(yb_ref, 0, buf_s.at[slot, 1], 0, sem.at[slot]).wait()
        return 0

    lax.fori_loop(0, tm, wait_row, 0)
    meta = meta_ref[...]
    g1 = meta[:, META_G:META_G + 1]
    g2 = meta[:, META_G + 1:META_G + 2]
    o_ref[...] = x_ref[...] + (buf_s[slot, 0] * g1 + buf_s[slot, 1] * g2)


def _combine(pos, x, meta, yb):
    t, d = x.shape
    tm = min(GATHER_TILE, t)
    return pl.pallas_call(
        functools.partial(_combine_kernel, tm=tm),
        grid_spec=pltpu.PrefetchScalarGridSpec(
            num_scalar_prefetch=1,
            grid=(t // tm,),
            in_specs=[
                pl.BlockSpec((tm, d), lambda i, pos: (i, 0)),
                pl.BlockSpec((tm, LANES), lambda i, pos: (i, 0)),
                pl.BlockSpec(memory_space=pl.ANY),
            ],
            out_specs=pl.BlockSpec((tm, d), lambda i, pos: (i, 0)),
            scratch_shapes=[pltpu.VMEM((2, 2, tm, d), F32), pltpu.SemaphoreType.DMA((2,))],
        ),
        out_shape=jax.ShapeDtypeStruct((t, d), F32),
        compiler_params=_cparams("arbitrary"),
        name="moe_combine",
    )(pos, x, meta, yb)


def _moe(x, g, w_rt, b_rt, w_gate, w_up, w_down, layer):
    t, d = x.shape
    meta, cnt = _router(x, g, w_rt, b_rt)
    bm = 256 if 2 * t >= 256 * N_EXPERTS * 4 else 128
    counts = cnt[0, :N_EXPERTS].astype(I32)
    padded = ((counts + bm - 1) // bm) * bm
    pend = jnp.cumsum(padded)
    pstart = pend - padded
    n_blocks = (2 * t + N_EXPERTS * (bm - 1) + bm - 1) // bm
    eid = meta[:, META_E:META_E + 2].astype(I32)
    rank = meta[:, META_R:META_R + 2].astype(I32)
    onehot = eid[:, :, None] == jnp.arange(N_EXPERTS, dtype=I32)[None, None, :]
    pos = (jnp.sum(jnp.where(onehot, pstart[None, None, :], 0), axis=-1) + rank).reshape(2 * t)
    blk_row0 = jnp.arange(n_blocks, dtype=I32) * bm
    blk_e = jnp.minimum(jnp.sum((pend[None, :] <= blk_row0[:, None]).astype(I32), axis=1), N_EXPERTS - 1)
    n_used = (pend[-1:] // bm).astype(I32)
    xb = _dispatch(pos, x, g, n_blocks * bm)
    yb = _experts(blk_e, n_used, xb, w_gate, w_up, w_down, layer, bm)
    return _combine(pos, x, meta, yb)


def _pad_lanes(a):
    return jnp.pad(a, [(0, 0)] * (a.ndim - 1) + [(0, LANES - a.shape[-1])])


def kernel(x_prompt, x_sample, mem_prompt, cache_fox_k, cache_fox_v, cache_fox_logf, state_conv_a, state_conv_c, cache_mem_k, cache_mem_v, norm_mix, norm_xmem, norm_mem, norm_ffn, even_w_in, even_w_conv_a, fox_q_gain, fox_k_gain, fox_f_bias, even_w_out, odd_w_in, odd_w_conv_c, conf_ln_gain, conf_ln_bias, sgu_v_gain, sgu_w, sgu_b, odd_w_out, xmem_wq, xmem_wk, xmem_wv, xmem_q_gain, xmem_k_gain, xmem_wo, moe_w_group, moe_b_group, moe_w_router, moe_b_router, moe_w_gate, moe_w_up, moe_w_down):
    bp, sp, d = x_prompt.shape
    bs, ss, _ = x_sample.shape
    depth = norm_mix.shape[0]
    n_mem = mem_prompt.shape[1]
    d_a = even_w_conv_a.shape[2]
    conv_a = even_w_conv_a.shape[1]
    conv_c = odd_w_conv_c.shape[1]
    d_c = odd_w_conv_c.shape[2]
    h_b = fox_f_bias.shape[1]
    d_b = h_b * DH
    d_x = xmem_wq.shape[2]
    h_mem = d_x // DH
    n_main = 3 * d_a + 3 * d_b

    xp = x_prompt.reshape(bp * sp, d)
    xs = x_sample.reshape(bs * ss, d)
    mem2 = mem_prompt.reshape(bp * n_mem, d)
    outs = {k: [] for k in ('fk_p', 'fv_p', 'fl_p', 'fk_s', 'fv_s', 'fl_s', 'ca_p', 'ca_s', 'cc_p', 'cc_s', 'cv_s', 'mk_p', 'mv_p')}

    for i in range(depth):
        g_mix = norm_mix[i][None]
        if i % 2 == 0:
            e = i // 2
            w_in = even_w_in[e]
            w_main = w_in[:, :n_main].astype(BF16)
            w_f = _pad_lanes(w_in[:, n_main:]).astype(BF16)
            f_bias = _pad_lanes(fox_f_bias[e][None])
            w_out = even_w_out[e].astype(BF16)
            args = (g_mix, w_main, w_f, f_bias, even_w_conv_a[e], fox_q_gain[e][None], fox_k_gain[e][None])

            ya, q, k, kb, v, vb, lf, st = _even_in(xp, *args, jnp.zeros((bp, conv_a - 1, d_a), F32), sp)
            lf3 = lf.reshape(bp, sp, LANES)
            c = _cumsum_time(lf3)
            c_rows = jnp.swapaxes(c[:, :, :h_b], 1, 2)[:, :, None, :]
            yb = _fox_prompt(q.reshape(bp, sp, d_b), kb.reshape(bp, sp, d_b), vb.reshape(bp, sp, d_b), c, c_rows)
            xp = _out_proj(ya, yb.reshape(bp * sp, d_b), w_out, xp)
            outs['fk_p'].append(k.reshape(bp, sp, h_b, DH))
            outs['fv_p'].append(v.reshape(bp, sp, h_b, DH))
            outs['fl_p'].append(lf3[:, :, :h_b])
            outs['ca_p'].append(st)

            ya, q, k, kb, v, vb, lf, st = _even_in(xs, *args, state_conv_a[e], ss)
            lf3 = lf.reshape(bs, ss, LANES)
            lf_all = jnp.concatenate([_pad_lanes(cache_fox_logf[e]), lf3], axis=1)
            c = _cumsum_time(lf_all)
            c_rows = jnp.swapaxes(c[:, :, :h_b], 1, 2)[:, :, None, :]
            yb = _fox_sample(q.reshape(bs, ss, d_b), cache_fox_k, cache_fox_v, e, kb.reshape(bs, ss, d_b),
                             vb.reshape(bs, ss, d_b), c, c_rows)
            xs = _out_proj(ya, yb.reshape(bs * ss, d_b), w_out, xs)
            outs['fk_s'].append(k.reshape(bs, ss, h_b, DH))
            outs['fv_s'].append(v.reshape(bs, ss, h_b, DH))
            outs['fl_s'].append(lf3[:, :, :h_b])
            outs['ca_s'].append(st)
        else:
            o = i // 2
            w_in = odd_w_in[o].astype(BF16)
            w_out = odd_w_out[o].astype(BF16)
            tril = jnp.tril(jnp.ones(sgu_w.shape[2:], dtype=bool))
            w_s = jnp.where(tril[None], sgu_w[o], 0).astype(BF16)
            b_s = sgu_b[o][:, :, None]

            def odd(x, prev, seq_len):
                lc = min(CHUNK_MLP, seq_len)
                return _odd_in(x, g_mix, w_in, odd_w_conv_c[o], conf_ln_gain[o][None], conf_ln_bias[o][None],
                               sgu_v_gain[o][None], w_s[:, :lc, :lc], b_s[:, :lc], prev, seq_len)

            yc, yd, _, st = odd(xp, jnp.zeros((bp, conv_c - 1, d_c), F32), sp)
            xp = _out_proj(yc, yd, w_out, xp)
            outs['cc_p'].append(st)
            yc, yd, vn, st = odd(xs, state_conv_c[o], ss)
            xs = _out_proj(yc, yd, w_out, xs)
            outs['cc_s'].append(st)
            outs['cv_s'].append(vn.reshape(bs, ss, -1))

        g_x = norm_xmem[i][None]
        wq = xmem_wq[i].astype(BF16)
        wo = xmem_wo[i].astype(BF16)
        q_gain = xmem_q_gain[i][None]
        mk, mv = _mem_kv(mem2, norm_mem[i][None], xmem_wk[i].astype(BF16), xmem_wv[i].astype(BF16), xmem_k_gain[i][None])
        mk3 = mk.reshape(bp, n_mem, d_x)
        mv3 = mv.reshape(bp, n_mem, d_x)
        outs['mk_p'].append(mk3.reshape(bp, n_mem, h_mem, DH))
        outs['mv_p'].append(mv3.reshape(bp, n_mem, h_mem, DH))
        xp = _mem_attn(xp, g_x, wq, q_gain, mk3, mv3, wo, sp)
        xs = _mem_attn(xs, g_x, wq, q_gain, cache_mem_k[i].reshape(bs, n_mem, d_x),
                       cache_mem_v[i].reshape(bs, n_mem, d_x), wo, ss)

        g_f = norm_ffn[i][None]
        w_rt = _pad_lanes(jnp.concatenate([moe_w_router[i], moe_w_group[i]], axis=1)).astype(BF16)
        b_rt = _pad_lanes(jnp.concatenate([moe_b_router[i], moe_b_group[i]])[None])
        xp = _moe(xp, g_f, w_rt, b_rt, moe_w_gate, moe_w_up, moe_w_down, i)
        xs = _moe(xs, g_f, w_rt, b_rt, moe_w_gate, moe_w_up, moe_w_down, i)

    st = lambda key: jnp.stack(outs[key])
    return (xp.reshape(bp, sp, d), xs.reshape(bs, ss, d),
            st('fk_p'), st('fv_p'), st('fl_p'),
            st('fk_s'), st('fv_s'), st('fl_s'),
            st('ca_p'), st('ca_s'),
            st('cc_p'), st('cc_s'),
            st('cv_s'),
            st('mk_p'), st('mv_p'))
```

```python
import functools

import jax
import jax.numpy as jnp
from jax import lax
from jax.experimental import pallas as pl
from jax.experimental.pallas import tpu as pltpu

F32 = jnp.float32
BF16 = jnp.bfloat16
I32 = jnp.int32

EPS = 1e-6
DH = 128
LANES = 128
LOG2E = 1.4426950408889634
CHUNK_MLP = 128
N_GROUPS = 4
EXP_PER_GROUP = 8
N_EXPERTS = N_GROUPS * EXP_PER_GROUP
V7X_VMEM_LIMIT = 56 * 1024 * 1024
ROW_TILE = 512
ATT_TILE = 512
CACHE_TILE = 512
CONV_ROWS, CONV_COLS = 32, 256
GATHER_TILE = 256


def _cparams(*sem):
    return pltpu.CompilerParams(dimension_semantics=sem, vmem_limit_bytes=V7X_VMEM_LIMIT)


def _rms(x, g):
    return x * lax.rsqrt(jnp.mean(x * x, axis=-1, keepdims=True) + EPS) * g


def _row_tile(n_rows, seq_len, cap=ROW_TILE):
    if seq_len >= cap:
        assert seq_len % cap == 0
        return cap
    nb = max(1, cap // seq_len)
    n_seq = n_rows // seq_len
    while n_seq % nb:
        nb -= 1
    return nb * seq_len


def _even_in_kernel(x_ref, g_ref, w_ref, wf_ref, fb_ref, wc_ref, qg_ref, kg_ref, prev_ref,
                    ya_ref, q_ref, k_ref, kb_ref, v_ref, vb_ref, lf_ref, st_ref,
                    xn_s, ab_s, ac_s, ext_s, *, nb, lt, tiles_per_seq, n_heads):
    i = pl.program_id(0)
    j = pl.program_id(1)
    halo = 8

    @pl.when(j == 0)
    def _():
        xn = _rms(x_ref[...], g_ref[...]).astype(BF16)
        xn_s[...] = xn
        z = jnp.dot(xn, wf_ref[...], preferred_element_type=F32) + fb_ref[...]
        lf_ref[...] = jnp.minimum(z, 0.0) - jnp.log1p(jnp.exp(-jnp.abs(z)))

    acc = jnp.dot(xn_s[...], w_ref[...], preferred_element_type=F32)

    @pl.when(j == 0)
    def _():
        ab_s[...] = acc

    @pl.when(j == 1)
    def _():
        ac_s[...] = acc

    @pl.when(j == 2)
    def _():
        gated = ac_s[...] * acc
        for n in range(nb):
            ext_s[n, halo:halo + lt, :] = gated[n * lt:(n + 1) * lt, :]

        @pl.when(i % tiles_per_seq == 0)
        def _():
            for n in range(nb):
                ext_s[n, halo - 2:halo, :] = prev_ref[n]

        w = wc_ref[...]
        for n in range(nb):
            conv = (w[0:1, :] * ext_s[n, halo - 2:halo - 2 + lt, :]
                    + w[1:2, :] * ext_s[n, halo - 1:halo - 1 + lt, :]
                    + w[2:3, :] * ext_s[n, halo:halo + lt, :])
            ya_ref[n * lt:(n + 1) * lt, :] = (ab_s[n * lt:(n + 1) * lt, :] * conv).astype(BF16)
            tail = ext_s[n, halo - 2 + lt:halo + lt, :]
            st_ref[n] = tail
            ext_s[n, halo - 2:halo, :] = tail

    def head_norm(gain):
        for h in range(n_heads):
            seg = acc[:, h * DH:(h + 1) * DH]
            yield h, seg * lax.rsqrt(jnp.mean(seg * seg, axis=-1, keepdims=True) + EPS) * gain

    @pl.when(j == 3)
    def _():
        for h, qn in head_norm(qg_ref[...]):
            q_ref[:, h * DH:(h + 1) * DH] = (qn * (DH ** -0.5 * LOG2E)).astype(BF16)

    @pl.when(j == 4)
    def _():
        for h, kn in head_norm(kg_ref[...]):
            k_ref[:, h * DH:(h + 1) * DH] = kn
            kb_ref[:, h * DH:(h + 1) * DH] = kn.astype(BF16)

    @pl.when(j == 5)
    def _():
        v_ref[...] = acc
        vb_ref[...] = acc.astype(BF16)


def _even_in(x, g, w_main, w_f, f_bias, w_conv, q_gain, k_gain, prev, seq_len):
    t, d = x.shape
    dc = w_conv.shape[1]
    n_col = w_main.shape[1] // dc
    assert n_col == 6
    n_heads = dc // DH
    tm = _row_tile(t, seq_len)
    lt = min(seq_len, tm)
    nb = tm // lt
    tiles_per_seq = seq_len // lt
    n_seq = t // seq_len
    kern = functools.partial(_even_in_kernel, nb=nb, lt=lt, tiles_per_seq=tiles_per_seq, n_heads=n_heads)
    row = lambda i, j: (i, 0)
    const = lambda i, j: (0, 0)
    seq3 = lambda i, j: (i // tiles_per_seq, 0, 0)
    return pl.pallas_call(
        kern,
        grid=(t // tm, n_col),
        in_specs=[
            pl.BlockSpec((tm, d), row),
            pl.BlockSpec((1, d), const),
            pl.BlockSpec((d, dc), lambda i, j: (0, j)),
            pl.BlockSpec((d, LANES), const),
            pl.BlockSpec((1, LANES), const),
            pl.BlockSpec(w_conv.shape, const),
            pl.BlockSpec((1, DH), const),
            pl.BlockSpec((1, DH), const),
            pl.BlockSpec((nb, 2, dc), seq3),
        ],
        out_specs=[
            pl.BlockSpec((tm, dc), row),
            pl.BlockSpec((tm, dc), row),
            pl.BlockSpec((tm, dc), row),
            pl.BlockSpec((tm, dc), row),
            pl.BlockSpec((tm, dc), row),
            pl.BlockSpec((tm, dc), row),
            pl.BlockSpec((tm, LANES), row),
            pl.BlockSpec((nb, 2, dc), seq3),
        ],
        out_shape=[
            jax.ShapeDtypeStruct((t, dc), BF16),
            jax.ShapeDtypeStruct((t, dc), BF16),
            jax.ShapeDtypeStruct((t, dc), F32),
            jax.ShapeDtypeStruct((t, dc), BF16),
            jax.ShapeDtypeStruct((t, dc), F32),
            jax.ShapeDtypeStruct((t, dc), BF16),
            jax.ShapeDtypeStruct((t, LANES), F32),
            jax.ShapeDtypeStruct((n_seq, 2, dc), F32),
        ],
        scratch_shapes=[
            pltpu.VMEM((tm, d), BF16),
            pltpu.VMEM((tm, dc), F32),
            pltpu.VMEM((tm, dc), F32),
            pltpu.VMEM((nb, 8 + lt, dc), F32),
        ],
        compiler_params=_cparams("arbitrary", "arbitrary"),
        name="even_in",
    )(x, g, w_main, w_f, f_bias, w_conv, q_gain, k_gain, prev)


def _cumsum_kernel(lf_ref, c_ref, carry_s, *, ts):
    @pl.when(pl.program_id(1) == 0)
    def _():
        carry_s[...] = jnp.zeros_like(carry_s)

    x = lf_ref[0]
    row = lax.broadcasted_iota(I32, x.shape, 0)
    d = 1
    while d < ts:
        x = x + jnp.where(row >= d, pltpu.roll(x, d, axis=0), 0.0)
        d *= 2
    x = x + carry_s[...]
    c_ref[0] = x * LOG2E
    carry_s[...] = x[ts - 1:ts, :]


def _time_tile(s, cap=1024):
    ts = min(s, cap)
    while s % ts or ts % 8:
        ts -= 1
    return ts


def _cumsum_time(lf):
    b, s, _ = lf.shape
    ts = _time_tile(s)
    return pl.pallas_call(
        functools.partial(_cumsum_kernel, ts=ts),
        grid=(b, s // ts),
        in_specs=[pl.BlockSpec((1, ts, LANES), lambda bi, si: (bi, si, 0))],
        out_specs=pl.BlockSpec((1, ts, LANES), lambda bi, si: (bi, si, 0)),
        out_shape=jax.ShapeDtypeStruct(lf.shape, F32),
        scratch_shapes=[pltpu.VMEM((1, LANES), F32)],
        compiler_params=_cparams("arbitrary", "arbitrary"),
        name="cumsum_time",
    )(lf)


def _online_softmax_step(s, v, m, l, acc):
    m_new = jnp.maximum(m, jnp.max(s, axis=1, keepdims=True))
    p = jnp.exp2(s - m_new)
    alpha = jnp.exp2(m - m_new)
    l = alpha * l + jnp.sum(p, axis=1, keepdims=True)
    acc = alpha * acc + jnp.dot(p.astype(BF16), v, preferred_element_type=F32)
    return m_new, l, acc


def _fox_prompt_kernel(q_ref, k_ref, v_ref, cq_ref, ck_ref, o_ref, *, tq):
    h = pl.program_id(1)
    qi = pl.program_id(2)
    q = q_ref[0]
    lane = lax.broadcasted_iota(I32, (tq, LANES), 1)
    cq = jnp.sum(jnp.where(lane == h, cq_ref[0], 0.0), axis=1, keepdims=True)

    def scores(kj):
        ks = pl.multiple_of(kj * tq, tq)
        k = k_ref[0, pl.ds(ks, tq), :]
        v = v_ref[0, pl.ds(ks, tq), :]
        s = lax.dot_general(q, k, (((1,), (1,)), ((), ())), preferred_element_type=F32)
        return s + (cq - ck_ref[0, 0, :, pl.ds(ks, tq)]), v

    def past_block(kj, carry):
        s, v = scores(kj)
        return _online_softmax_step(s, v, *carry)

    init = (jnp.full((tq, 1), -jnp.inf, F32), jnp.zeros((tq, 1), F32), jnp.zeros((tq, DH), F32))
    m, l, acc = lax.fori_loop(0, qi, past_block, init)
    s, v = scores(qi)
    r = lax.broadcasted_iota(I32, (tq, tq), 0)
    c = lax.broadcasted_iota(I32, (tq, tq), 1)
    s = jnp.where(c <= r, s, -jnp.inf)
    m, l, acc = _online_softmax_step(s, v, m, l, acc)
    o_ref[0] = (acc / l).astype(BF16)


def _fox_prompt(q, k, v, c_pad, c_rows):
    b, s, hd = q.shape
    n_heads = hd // DH
    tq = min(ATT_TILE, s)
    assert s % tq == 0
    return pl.pallas_call(
        functools.partial(_fox_prompt_kernel, tq=tq),
        grid=(b, n_heads, s // tq),
        in_specs=[
            pl.BlockSpec((1, tq, DH), lambda bi, h, qi: (bi, qi, h)),
            pl.BlockSpec((1, s, DH), lambda bi, h, qi: (bi, 0, h)),
            pl.BlockSpec((1, s, DH), lambda bi, h, qi: (bi, 0, h)),
            pl.BlockSpec((1, tq, LANES), lambda bi, h, qi: (bi, qi, 0)),
            pl.BlockSpec((1, 1, 1, s), lambda bi, h, qi: (bi, h, 0, 0)),
        ],
        out_specs=pl.BlockSpec((1, tq, DH), lambda bi, h, qi: (bi, qi, h)),
        out_shape=jax.ShapeDtypeStruct((b, s, hd), BF16),
        compiler_params=_cparams("arbitrary", "arbitrary", "arbitrary"),
        name="fox_prompt",
    )(q, k, v, c_pad, c_rows)


def _fox_sample_kernel(q_ref, ck_ref, cv_ref, kn_ref, vn_ref, cq_ref, cr_ref, o_ref,
                       m_s, l_s, acc_s, *, n_heads, t, tk, past):
    kj = pl.program_id(1)
    last = pl.num_programs(1) - 1

    @pl.when(kj == 0)
    def _():
        m_s[...] = jnp.full_like(m_s, -jnp.inf)
        l_s[...] = jnp.zeros_like(l_s)
        acc_s[...] = jnp.zeros_like(acc_s)

    lane = lax.broadcasted_iota(I32, (t, LANES), 1)
    ks = pl.multiple_of(kj * tk, tk)

    def head_q(h):
        cq = jnp.sum(jnp.where(lane == h, cq_ref[0], 0.0), axis=1, keepdims=True)
        return q_ref[0, :, h * DH:(h + 1) * DH], cq

    def softmax_step(scores, values, m, l, acc):
        s = jnp.concatenate(scores, axis=0)
        m_new = jnp.maximum(m, jnp.max(s, axis=1, keepdims=True))
        p = jnp.exp2(s - m_new)
        alpha = jnp.exp2(m - m_new)
        pb = p.astype(BF16)
        pv = jnp.concatenate([jnp.dot(pb[h * t:(h + 1) * t], values[h], preferred_element_type=F32)
                              for h in range(n_heads)], axis=0)
        return m_new, alpha * l + jnp.sum(p, axis=1, keepdims=True), alpha * acc + pv

    scores, values = [], []
    for h in range(n_heads):
        q, cq = head_q(h)
        k = ck_ref[pl.ds(h, tk, stride=n_heads), :].astype(BF16)
        s = lax.dot_general(q, k, (((1,), (1,)), ((), ())), preferred_element_type=F32)
        scores.append(s + (cq - cr_ref[0, h, :, pl.ds(ks, tk)]))
        values.append(cv_ref[pl.ds(h, tk, stride=n_heads), :].astype(BF16))
    m, l, acc = softmax_step(scores, values, m_s[...], l_s[...], acc_s[...])
    m_s[...] = m
    l_s[...] = l
    acc_s[...] = acc

    @pl.when(kj == last)
    def _():
        r = lax.broadcasted_iota(I32, (t, t), 0)
        c = lax.broadcasted_iota(I32, (t, t), 1)
        scores, values = [], []
        for h in range(n_heads):
            sl = slice(h * DH, (h + 1) * DH)
            q, cq = head_q(h)
            s = lax.dot_general(q, kn_ref[0, :, sl], (((1,), (1,)), ((), ())), preferred_element_type=F32)
            scores.append(jnp.where(c <= r, s + (cq - cr_ref[0, h, :, past:past + t]), -jnp.inf))
            values.append(vn_ref[0, :, sl])
        _, l2, acc2 = softmax_step(scores, values, m, l, acc)
        out = acc2 / l2
        for h in range(n_heads):
            o_ref[0, :, h * DH:(h + 1) * DH] = out[h * t:(h + 1) * t].astype(BF16)


def _fox_sample(q, cache_k, cache_v, layer, k_new, v_new, c_pad, c_rows):
    b, t, hd = q.shape
    n_heads = hd // DH
    past = cache_k.shape[2]
    tk = min(CACHE_TILE, past)
    assert past % tk == 0 and past % t == 0 and tk % LANES == 0
    assert cache_k.shape[1:] == (b, past, n_heads, DH)
    kern = functools.partial(_fox_sample_kernel, n_heads=n_heads, t=t, tk=tk, past=past)
    per_b = lambda bi, kj: (bi, 0, 0)
    n_kb = past // tk
    cache_rows = lambda bi, kj: ((layer * b + bi) * n_kb + kj, 0)
    return pl.pallas_call(
        kern,
        grid=(b, n_kb),
        in_specs=[
            pl.BlockSpec((1, t, hd), per_b),
            pl.BlockSpec((tk * n_heads, DH), cache_rows),
            pl.BlockSpec((tk * n_heads, DH), cache_rows),
            pl.BlockSpec((1, t, hd), per_b),
            pl.BlockSpec((1, t, hd), per_b),
            pl.BlockSpec((1, t, LANES), lambda bi, kj: (bi, past // t, 0)),
            pl.BlockSpec((1, n_heads, 1, past + t), lambda bi, kj: (bi, 0, 0, 0)),
        ],
        out_specs=pl.BlockSpec((1, t, hd), per_b),
        out_shape=jax.ShapeDtypeStruct((b, t, hd), BF16),
        scratch_shapes=[
            pltpu.VMEM((n_heads * t, 1), F32),
            pltpu.VMEM((n_heads * t, 1), F32),
            pltpu.VMEM((n_heads * t, DH), F32),
        ],
        compiler_params=_cparams("arbitrary", "arbitrary"),
        name="fox_sample",
    )(q, cache_k.reshape(-1, DH), cache_v.reshape(-1, DH), k_new, v_new, c_pad, c_rows)


def _out_proj_kernel(a_ref, b_ref, wa_ref, wb_ref, r_ref, o_ref):
    o_ref[...] = (r_ref[...]
                  + jnp.dot(a_ref[...], wa_ref[...], preferred_element_type=F32)
                  + jnp.dot(b_ref[...], wb_ref[...], preferred_element_type=F32))


def _out_proj(a, b, w, res):
    t, ka = a.shape
    kb = b.shape[1]
    d = w.shape[1]
    tm = _row_tile(t, t)
    assert ka == kb
    return pl.pallas_call(
        _out_proj_kernel,
        grid=(t // tm,),
        in_specs=[
            pl.BlockSpec((tm, ka), lambda i: (i, 0)),
            pl.BlockSpec((tm, kb), lambda i: (i, 0)),
            pl.BlockSpec((ka, d), lambda i: (0, 0)),
            pl.BlockSpec((kb, d), lambda i: (1, 0)),
            pl.BlockSpec((tm, d), lambda i: (i, 0)),
        ],
        out_specs=pl.BlockSpec((tm, d), lambda i: (i, 0)),
        out_shape=jax.ShapeDtypeStruct((t, d), F32),
        compiler_params=_cparams("arbitrary"),
        name="out_proj",
    )(a, b, w, w, res)


def _odd_in_kernel(x_ref, g_ref, w_ref, wc_ref, lg_ref, lb_ref, vg_ref, ws_ref, bs_ref, prev_ref,
                   yc_ref, yd_ref, vn_ref, st_ref,
                   xn_s, val_s, u_s, ext_s, conv_s, *, nb, lt, tiles_per_seq, taps, n_groups, lc):
    i = pl.program_id(0)
    j = pl.program_id(1)
    hist = taps - 1
    halo = ((hist + 7) // 8) * 8

    @pl.when(j == 0)
    def _():
        xn_s[...] = _rms(x_ref[...], g_ref[...]).astype(BF16)

    acc = jnp.dot(xn_s[...], w_ref[...], preferred_element_type=F32)

    @pl.when(j == 0)
    def _():
        val_s[...] = acc

    @pl.when(j == 1)
    def _():
        glu = val_s[...] * jax.nn.sigmoid(acc)
        for n in range(nb):
            ext_s[n, halo:halo + lt, :] = glu[n * lt:(n + 1) * lt, :]

        @pl.when(i % tiles_per_seq == 0)
        def _():
            for n in range(nb):
                ext_s[n, halo - hist:halo, :] = prev_ref[n]
                if halo > hist:
                    ext_s[n, 0:halo - hist, :] = jnp.zeros((halo - hist, ext_s.shape[2]), F32)

        dc = conv_s.shape[1]
        off0 = halo - hist
        win = ((off0 + hist + CONV_ROWS + 7) // 8) * 8
        for n in range(nb):
            for r0 in range(0, lt, CONV_ROWS):
                for c0 in range(0, dc, CONV_COLS):
                    cols = slice(c0, c0 + CONV_COLS)
                    blk = ext_s[n, r0:r0 + win, cols]
                    part = None
                    for b in range(8):
                        rolled = blk if b == 0 else pltpu.roll(blk, win - b, axis=0)
                        for a in range(win // 8):
                            s = 8 * a + b - off0
                            if 0 <= s < taps and 8 * a + CONV_ROWS <= win:
                                term = wc_ref[s:s + 1, cols] * rolled[8 * a:8 * a + CONV_ROWS, :]
                                part = term if part is None else part + term
                    conv_s[n * lt + r0:n * lt + r0 + CONV_ROWS, cols] = part
        for n in range(nb):
            conv = conv_s[n * lt:(n + 1) * lt, :]
            mu = jnp.mean(conv, axis=-1, keepdims=True)
            xc = conv - mu
            var = jnp.mean(xc * xc, axis=-1, keepdims=True)
            y = xc * lax.rsqrt(var + EPS) * lg_ref[...] + lb_ref[...]
            yc_ref[n * lt:(n + 1) * lt, :] = (y * jax.nn.sigmoid(y)).astype(BF16)
            tail = ext_s[n, halo - hist + lt:halo + lt, :]
            st_ref[n] = tail
            ext_s[n, halo - hist:halo, :] = tail

    @pl.when(j == 2)
    def _():
        u_s[...] = acc

    @pl.when(j == 3)
    def _():
        vn = _rms(acc, vg_ref[...])
        vn_ref[...] = vn
        vb = vn.astype(BF16)
        gw = vn.shape[1] // n_groups
        for c in range(vn.shape[0] // lc):
            rows = slice(c * lc, (c + 1) * lc)
            for g in range(n_groups):
                cols = slice(g * gw, (g + 1) * gw)
                gate = jnp.dot(ws_ref[g], vb[rows, cols], preferred_element_type=F32) + bs_ref[g]
                yd_ref[rows, cols] = (u_s[rows, cols] * gate).astype(BF16)


def _odd_in(x, g, w_in, w_conv, ln_g, ln_b, v_gain, w_s, b_s, prev, seq_len):
    t, d = x.shape
    taps, dc = w_conv.shape
    assert w_in.shape[1] == 4 * dc
    n_groups = w_s.shape[0]
    tm = _row_tile(t, seq_len, cap=ROW_TILE if seq_len >= ROW_TILE else ROW_TILE // 2)
    lt = min(seq_len, tm)
    nb = tm // lt
    tiles_per_seq = seq_len // lt
    n_seq = t // seq_len
    lc = min(CHUNK_MLP, seq_len)
    assert lt % lc == 0 and lt % CONV_ROWS == 0 and dc % CONV_COLS == 0
    hist = taps - 1
    halo = ((hist + 7) // 8) * 8
    kern = functools.partial(_odd_in_kernel, nb=nb, lt=lt, tiles_per_seq=tiles_per_seq, taps=taps,
                             n_groups=n_groups, lc=lc)
    row = lambda i, j: (i, 0)
    const = lambda i, j: (0, 0)
    const3 = lambda i, j: (0, 0, 0)
    seq3 = lambda i, j: (i // tiles_per_seq, 0, 0)
    return pl.pallas_call(
        kern,
        grid=(t // tm, 4),
        in_specs=[
            pl.BlockSpec((tm, d), row),
            pl.BlockSpec((1, d), const),
            pl.BlockSpec((d, dc), lambda i, j: (0, j)),
            pl.BlockSpec((taps, dc), const),
            pl.BlockSpec((1, dc), const),
            pl.BlockSpec((1, dc), const),
            pl.BlockSpec((1, dc), const),
            pl.BlockSpec((n_groups, lc, lc), const3),
            pl.BlockSpec((n_groups, lc, 1), const3),
            pl.BlockSpec((nb, hist, dc), seq3),
        ],
        out_specs=[
            pl.BlockSpec((tm, dc), row),
            pl.BlockSpec((tm, dc), row),
            pl.BlockSpec((tm, dc), row),
            pl.BlockSpec((nb, hist, dc), seq3),
        ],
        out_shape=[
            jax.ShapeDtypeStruct((t, dc), BF16),
            jax.ShapeDtypeStruct((t, dc), BF16),
            jax.ShapeDtypeStruct((t, dc), F32),
            jax.ShapeDtypeStruct((n_seq, hist, dc), F32),
        ],
        scratch_shapes=[
            pltpu.VMEM((tm, d), BF16),
            pltpu.VMEM((tm, dc), F32),
            pltpu.VMEM((tm, dc), F32),
            pltpu.VMEM((nb, halo + lt, dc), F32),
            pltpu.VMEM((tm, dc), F32),
        ],
        compiler_params=_cparams("arbitrary", "arbitrary"),
        name="odd_in",
    )(x, g, w_in, w_conv, ln_g, ln_b, v_gain, w_s, b_s, prev)


def _mem_kv_kernel(m_ref, g_ref, wk_ref, wv_ref, kg_ref, k_ref, v_ref, *, n_heads):
    mn = _rms(m_ref[...], g_ref[...]).astype(BF16)
    k = jnp.dot(mn, wk_ref[...], preferred_element_type=F32)
    for h in range(n_heads):
        seg = k[:, h * DH:(h + 1) * DH]
        k_ref[:, h * DH:(h + 1) * DH] = seg * lax.rsqrt(jnp.mean(seg * seg, axis=-1, keepdims=True) + EPS) * kg_ref[...]
    v_ref[...] = jnp.dot(mn, wv_ref[...], preferred_element_type=F32)


def _mem_kv(mem, g, wk, wv, k_gain):
    t, d = mem.shape
    dx = wk.shape[1]
    tm = _row_tile(t, t)
    return pl.pallas_call(
        functools.partial(_mem_kv_kernel, n_heads=dx // DH),
        grid=(t // tm,),
        in_specs=[
            pl.BlockSpec((tm, d), lambda i: (i, 0)),
            pl.BlockSpec((1, d), lambda i: (0, 0)),
            pl.BlockSpec((d, dx), lambda i: (0, 0)),
            pl.BlockSpec((d, dx), lambda i: (0, 0)),
            pl.BlockSpec((1, DH), lambda i: (0, 0)),
        ],
        out_specs=[pl.BlockSpec((tm, dx), lambda i: (i, 0))] * 2,
        out_shape=[jax.ShapeDtypeStruct((t, dx), F32)] * 2,
        compiler_params=_cparams("arbitrary"),
        name="mem_kv",
    )(mem, g, wk, wv, k_gain)


def _mem_attn_kernel(x_ref, g_ref, wq_ref, qg_ref, k_ref, v_ref, wo_ref, o_ref, att_s, *, nb, lt, n_heads):
    x = x_ref[...]
    xn = _rms(x, g_ref[...]).astype(BF16)
    q = jnp.dot(xn, wq_ref[...], preferred_element_type=F32)
    for h in range(n_heads):
        sl = slice(h * DH, (h + 1) * DH)
        seg = q[:, sl]
        qn = seg * lax.rsqrt(jnp.mean(seg * seg, axis=-1, keepdims=True) + EPS) * qg_ref[...]
        qb = (qn * (DH ** -0.5)).astype(BF16)
        for n in range(nb):
            rows = slice(n * lt, (n + 1) * lt)
            k = k_ref[n, :, sl].astype(BF16)
            v = v_ref[n, :, sl].astype(BF16)
            s = lax.dot_general(qb[rows], k, (((1,), (1,)), ((), ())), preferred_element_type=F32)
            p = jnp.exp(s - jnp.max(s, axis=1, keepdims=True))
            o = jnp.dot(p.astype(BF16), v, preferred_element_type=F32) / jnp.sum(p, axis=1, keepdims=True)
            att_s[rows, sl] = o.astype(BF16)
    o_ref[...] = x + jnp.dot(att_s[...], wo_ref[...], preferred_element_type=F32)


def _mem_attn(x, g, wq, q_gain, k, v, wo, seq_len):
    t, d = x.shape
    dx = wq.shape[1]
    n_mem = k.shape[1]
    tm = _row_tile(t, seq_len, cap=256 if seq_len < 256 else ROW_TILE)
    lt = min(seq_len, tm)
    nb = tm // lt
    tiles_per_seq = seq_len // lt
    kern = functools.partial(_mem_attn_kernel, nb=nb, lt=lt, n_heads=dx // DH)
    return pl.pallas_call(
        kern,
        grid=(t // tm,),
        in_specs=[
            pl.BlockSpec((tm, d), lambda i: (i, 0)),
            pl.BlockSpec((1, d), lambda i: (0, 0)),
            pl.BlockSpec((d, dx), lambda i: (0, 0)),
            pl.BlockSpec((1, DH), lambda i: (0, 0)),
            pl.BlockSpec((nb, n_mem, dx), lambda i: (i // tiles_per_seq, 0, 0)),
            pl.BlockSpec((nb, n_mem, dx), lambda i: (i // tiles_per_seq, 0, 0)),
            pl.BlockSpec((dx, d), lambda i: (0, 0)),
        ],
        out_specs=pl.BlockSpec((tm, d), lambda i: (i, 0)),
        out_shape=jax.ShapeDtypeStruct((t, d), F32),
        scratch_shapes=[pltpu.VMEM((tm, dx), BF16)],
        compiler_params=_cparams("arbitrary"),
        name="mem_attn",
    )(x, g, wq, q_gain, k, v, wo)


GROUP_LANE0 = N_EXPERTS
META_E, META_G, META_R = 0, 2, 4


def _router_kernel(x_ref, g_ref, w_ref, b_ref, meta_ref, cnt_ref, *, tm):
    @pl.when(pl.program_id(0) == 0)
    def _():
        cnt_ref[...] = jnp.zeros_like(cnt_ref)

    xn = _rms(x_ref[...], g_ref[...]).astype(BF16)
    logits = jnp.dot(xn, w_ref[...], preferred_element_type=F32) + b_ref[...]
    lane = lax.broadcasted_iota(I32, (tm, LANES), 1)
    big = jnp.int32(LANES)

    def first_argmax(vals):
        top = jnp.max(vals, axis=1, keepdims=True)
        return top, jnp.min(jnp.where(vals == top, lane, big), axis=1, keepdims=True)

    is_grp = (lane >= GROUP_LANE0) & (lane < GROUP_LANE0 + N_GROUPS)
    lg = jnp.where(is_grp, logits, -jnp.inf)
    g_top, g_lane = first_argmax(lg)
    p_sel = 1.0 / jnp.sum(jnp.where(is_grp, jnp.exp(lg - g_top), 0.0), axis=1, keepdims=True)
    e0 = (g_lane - GROUP_LANE0) * EXP_PER_GROUP
    le = jnp.where((lane >= e0) & (lane < e0 + EXP_PER_GROUP), logits, -jnp.inf)
    v1, i1 = first_argmax(le)
    v2, i2 = first_argmax(jnp.where(lane == i1, -jnp.inf, le))
    e = jnp.exp(v2 - v1)
    g1 = p_sel / (1.0 + e)
    g2 = p_sel * e / (1.0 + e)

    sel1 = lane == i1
    sel2 = lane == i2
    onehot = jnp.where(sel1 | sel2, 1.0, 0.0)
    r = lax.broadcasted_iota(I32, (tm, tm), 0)
    c = lax.broadcasted_iota(I32, (tm, tm), 1)
    earlier = jnp.where(c < r, 1.0, 0.0).astype(BF16)
    before = jnp.dot(earlier, onehot.astype(BF16), preferred_element_type=F32) + cnt_ref[...]
    r1 = jnp.sum(jnp.where(sel1, before, 0.0), axis=1, keepdims=True)
    r2 = jnp.sum(jnp.where(sel2, before, 0.0), axis=1, keepdims=True)
    cnt_ref[...] = cnt_ref[...] + jnp.sum(onehot, axis=0, keepdims=True)

    meta = jnp.zeros((tm, LANES), F32)
    for k, val in ((META_E, i1.astype(F32)), (META_E + 1, i2.astype(F32)), (META_G, g1), (META_G + 1, g2),
                   (META_R, r1), (META_R + 1, r2)):
        meta = jnp.where(lane == k, val, meta)
    meta_ref[...] = meta


def _router(x, g, w_rt, b_rt):
    t, d = x.shape
    tm = _row_tile(t, t)
    return pl.pallas_call(
        functools.partial(_router_kernel, tm=tm),
        grid=(t // tm,),
        in_specs=[
            pl.BlockSpec((tm, d), lambda i: (i, 0)),
            pl.BlockSpec((1, d), lambda i: (0, 0)),
            pl.BlockSpec((d, LANES), lambda i: (0, 0)),
            pl.BlockSpec((1, LANES), lambda i: (0, 0)),
        ],
        out_specs=[pl.BlockSpec((tm, LANES), lambda i: (i, 0)), pl.BlockSpec((1, LANES), lambda i: (0, 0))],
        out_shape=[jax.ShapeDtypeStruct((t, LANES), F32), jax.ShapeDtypeStruct((1, LANES), F32)],
        compiler_params=_cparams("arbitrary"),
        name="moe_router",
    )(x, g, w_rt, b_rt)


def _row_copy(src, src_row, dst, dst_row, sem, nc):
    def rows(r):
        return pl.ds(r * nc if isinstance(r, int) else pl.multiple_of(r * nc, nc), nc)

    return pltpu.make_async_copy(src.at[rows(src_row)], dst.at[rows(dst_row)], sem)


def _rows_to_chunks(ref, val, nc):
    n = val.shape[0]
    for c in range(nc):
        ref[pl.ds(c, n, stride=nc), :] = val[:, c * LANES:(c + 1) * LANES]


def _chunks_to_rows(ref, n, nc):
    return jnp.concatenate([ref[pl.ds(c, n, stride=nc), :] for c in range(nc)], axis=1)


def _dispatch_kernel(pos_ref, x_ref, g_ref, zero_ref, xb_ref, xn_s, sem, *, tm, nc):
    del zero_ref
    i = pl.program_id(0)
    n_steps = pl.num_programs(0)
    slot = i % 2
    base = i * (2 * tm)

    def drain(sl):
        def wait_row(r, _):
            _row_copy(xn_s.at[sl], 0, xb_ref, 0, sem.at[sl], nc).wait()
            _row_copy(xn_s.at[sl], 0, xb_ref, 0, sem.at[sl], nc).wait()
            return 0

        lax.fori_loop(0, tm, wait_row, 0)

    @pl.when(i >= 2)
    def _():
        drain(slot)

    _rows_to_chunks(xn_s.at[slot], _rms(x_ref[...], g_ref[...]), nc)

    def issue(r, _):
        _row_copy(xn_s.at[slot], r, xb_ref, pos_ref[base + 2 * r], sem.at[slot], nc).start()
        _row_copy(xn_s.at[slot], r, xb_ref, pos_ref[base + 2 * r + 1], sem.at[slot], nc).start()
        return 0

    lax.fori_loop(0, tm, issue, 0)

    @pl.when(i == n_steps - 1)
    def _():
        @pl.when(n_steps >= 2)
        def _():
            drain(1 - slot)

        drain(slot)


def _dispatch(pos, x, g, n_rows):
    t, d = x.shape
    tm = min(GATHER_TILE, t)
    nc = d // LANES
    assert t % tm == 0 and d % LANES == 0
    zeros = jnp.zeros((n_rows * nc, LANES), F32)
    return pl.pallas_call(
        functools.partial(_dispatch_kernel, tm=tm, nc=nc),
        grid_spec=pltpu.PrefetchScalarGridSpec(
            num_scalar_prefetch=1,
            grid=(t // tm,),
            in_specs=[
                pl.BlockSpec((tm, d), lambda i, pos: (i, 0)),
                pl.BlockSpec((1, d), lambda i, pos: (0, 0)),
                pl.BlockSpec(memory_space=pl.ANY),
            ],
            out_specs=pl.BlockSpec(memory_space=pl.ANY),
            scratch_shapes=[pltpu.VMEM((2, tm * nc, LANES), F32), pltpu.SemaphoreType.DMA((2,))],
        ),
        out_shape=jax.ShapeDtypeStruct((n_rows * nc, LANES), F32),
        input_output_aliases={3: 0},
        compiler_params=_cparams("arbitrary"),
        name="moe_dispatch",
    )(pos, x, g, zeros)


def _expert_kernel(be_ref, nu_ref, x_ref, wg_ref, wu_ref, wd_ref, y_ref, wg_s, wu_s, wd_s, *, bm, nc):
    b = pl.program_id(0)
    used = b < nu_ref[0]
    prev = be_ref[jnp.maximum(b - 1, 0)]

    @pl.when(used & ((b == 0) | (be_ref[b] != prev)))
    def _():
        wg_s[...] = wg_ref[0, 0].astype(BF16)
        wu_s[...] = wu_ref[0, 0].astype(BF16)
        wd_s[...] = wd_ref[0, 0].astype(BF16)

    @pl.when(used)
    def _():
        x = _chunks_to_rows(x_ref, bm, nc).astype(BF16)
        gate = jnp.dot(x, wg_s[...], preferred_element_type=F32)
        up = jnp.dot(x, wu_s[...], preferred_element_type=F32)
        hid = (gate * jax.nn.sigmoid(gate) * up).astype(BF16)
        _rows_to_chunks(y_ref, jnp.dot(hid, wd_s[...], preferred_element_type=F32), nc)

    @pl.when(jnp.logical_not(used))
    def _():
        y_ref[...] = jnp.zeros_like(y_ref)


def _experts(blk_e, n_used, xb, w_gate, w_up, w_down, layer, bm):
    de, d = w_down.shape[2:]
    nc = d // LANES
    p = xb.shape[0] // nc
    n_blocks = p // bm

    def xmap(b, be, nu):
        return (jnp.minimum(b, jnp.maximum(nu[0] - 1, 0)), 0)

    def wmap(b, be, nu):
        return (layer, be[b], 0, 0)

    return pl.pallas_call(
        functools.partial(_expert_kernel, bm=bm, nc=nc),
        grid_spec=pltpu.PrefetchScalarGridSpec(
            num_scalar_prefetch=2,
            grid=(n_blocks,),
            in_specs=[
                pl.BlockSpec((bm * nc, LANES), xmap),
                pl.BlockSpec((1, 1, d, de), wmap),
                pl.BlockSpec((1, 1, d, de), wmap),
                pl.BlockSpec((1, 1, de, d), wmap),
            ],
            out_specs=pl.BlockSpec((bm * nc, LANES), lambda b, be, nu: (b, 0)),
            scratch_shapes=[pltpu.VMEM((d, de), BF16), pltpu.VMEM((d, de), BF16), pltpu.VMEM((de, d), BF16)],
        ),
        out_shape=jax.ShapeDtypeStruct((p * nc, LANES), F32),
        compiler_params=_cparams("arbitrary"),
        name="moe_experts",
    )(blk_e, n_used, xb, w_gate, w_up, w_down)


def _combine_kernel(pos_ref, x_ref, meta_ref, yb_ref, o_ref, buf_s, sem, *, tm, nc):
    i = pl.program_id(0)
    slot = i % 2

    def issue(tile, sl):
        base = tile * (2 * tm)

        def start_row(r, _):
            _row_copy(yb_ref, pos_ref[base + 2 * r], buf_s.at[sl, 0], r, sem.at[sl], nc).start()
            _row_copy(yb_ref, pos_ref[base + 2 * r + 1], buf_s.at[sl, 1], r, sem.at[sl], nc).start()
            return 0

        lax.fori_loop(0, tm, start_row, 0)

    @pl.when(i == 0)
    def _():
        issue(0, 0)

    @pl.when(i + 1 < pl.num_programs(0))
    def _():
        issue(i + 1, 1 - slot)

    def wait_row(r, _):
        _row_copy(yb_ref, 0, buf_s.at[slot, 0], 0, sem.at[slot], nc).wait()
        _row_copy(yb_ref, 0, buf_s.at[slot, 1], 0, sem.at[slot], nc).wait()
        return 0

    lax.fori_loop(0, tm, wait_row, 0)
    meta = meta_ref[...]
    g1 = jnp.broadcast_to(meta[:, META_G:META_G + 1], (tm, LANES))
    g2 = jnp.broadcast_to(meta[:, META_G + 1:META_G + 2], (tm, LANES))
    for c in range(nc):
        cols = slice(c * LANES, (c + 1) * LANES)
        y1 = buf_s[slot, 0, pl.ds(c, tm, stride=nc), :]
        y2 = buf_s[slot, 1, pl.ds(c, tm, stride=nc), :]
        o_ref[:, cols] = x_ref[:, cols] + (y1 * g1 + y2 * g2)


def _combine(pos, x, meta, yb):
    t, d = x.shape
    tm = min(GATHER_TILE, t)
    nc = d // LANES
    return pl.pallas_call(
        functools.partial(_combine_kernel, tm=tm, nc=nc),
        grid_spec=pltpu.PrefetchScalarGridSpec(
            num_scalar_prefetch=1,
            grid=(t // tm,),
            in_specs=[
                pl.BlockSpec((tm, d), lambda i, pos: (i, 0)),
                pl.BlockSpec((tm, LANES), lambda i, pos: (i, 0)),
                pl.BlockSpec(memory_space=pl.ANY),
            ],
            out_specs=pl.BlockSpec((tm, d), lambda i, pos: (i, 0)),
            scratch_shapes=[pltpu.VMEM((2, 2, tm * nc, LANES), F32), pltpu.SemaphoreType.DMA((2,))],
        ),
        out_shape=jax.ShapeDtypeStruct((t, d), F32),
        compiler_params=_cparams("arbitrary"),
        name="moe_combine",
    )(pos, x, meta, yb)


def _moe(x, g, w_rt, b_rt, w_gate, w_up, w_down, layer):
    t, d = x.shape
    meta, cnt = _router(x, g, w_rt, b_rt)
    bm = 256 if 2 * t >= 256 * N_EXPERTS * 4 else 128
    counts = cnt[0, :N_EXPERTS].astype(I32)
    padded = ((counts + bm - 1) // bm) * bm
    pend = jnp.cumsum(padded)
    pstart = pend - padded
    n_blocks = (2 * t + N_EXPERTS * (bm - 1) + bm - 1) // bm
    eid = meta[:, META_E:META_E + 2].astype(I32)
    rank = meta[:, META_R:META_R + 2].astype(I32)
    onehot = eid[:, :, None] == jnp.arange(N_EXPERTS, dtype=I32)[None, None, :]
    pos = (jnp.sum(jnp.where(onehot, pstart[None, None, :], 0), axis=-1) + rank).reshape(2 * t)
    blk_row0 = jnp.arange(n_blocks, dtype=I32) * bm
    blk_e = jnp.minimum(jnp.sum((pend[None, :] <= blk_row0[:, None]).astype(I32), axis=1), N_EXPERTS - 1)
    n_used = (pend[-1:] // bm).astype(I32)
    xb = _dispatch(pos, x, g, n_blocks * bm)
    yb = _experts(blk_e, n_used, xb, w_gate, w_up, w_down, layer, bm)
    return _combine(pos, x, meta, yb)


def _pad_lanes(a):
    return jnp.pad(a, [(0, 0)] * (a.ndim - 1) + [(0, LANES - a.shape[-1])])


def kernel(x_prompt, x_sample, mem_prompt, cache_fox_k, cache_fox_v, cache_fox_logf, state_conv_a, state_conv_c, cache_mem_k, cache_mem_v, norm_mix, norm_xmem, norm_mem, norm_ffn, even_w_in, even_w_conv_a, fox_q_gain, fox_k_gain, fox_f_bias, even_w_out, odd_w_in, odd_w_conv_c, conf_ln_gain, conf_ln_bias, sgu_v_gain, sgu_w, sgu_b, odd_w_out, xmem_wq, xmem_wk, xmem_wv, xmem_q_gain, xmem_k_gain, xmem_wo, moe_w_group, moe_b_group, moe_w_router, moe_b_router, moe_w_gate, moe_w_up, moe_w_down):
    bp, sp, d = x_prompt.shape
    bs, ss, _ = x_sample.shape
    depth = norm_mix.shape[0]
    n_mem = mem_prompt.shape[1]
    d_a = even_w_conv_a.shape[2]
    conv_a = even_w_conv_a.shape[1]
    conv_c = odd_w_conv_c.shape[1]
    d_c = odd_w_conv_c.shape[2]
    h_b = fox_f_bias.shape[1]
    d_b = h_b * DH
    d_x = xmem_wq.shape[2]
    h_mem = d_x // DH
    n_main = 3 * d_a + 3 * d_b

    xp = x_prompt.reshape(bp * sp, d)
    xs = x_sample.reshape(bs * ss, d)
    mem2 = mem_prompt.reshape(bp * n_mem, d)
    outs = {k: [] for k in ('fk_p', 'fv_p', 'fl_p', 'fk_s', 'fv_s', 'fl_s', 'ca_p', 'ca_s', 'cc_p', 'cc_s', 'cv_s', 'mk_p', 'mv_p')}

    for i in range(depth):
        g_mix = norm_mix[i][None]
        if i % 2 == 0:
            e = i // 2
            w_in = even_w_in[e]
            w_main = w_in[:, :n_main].astype(BF16)
            w_f = _pad_lanes(w_in[:, n_main:]).astype(BF16)
            f_bias = _pad_lanes(fox_f_bias[e][None])
            w_out = even_w_out[e].astype(BF16)
            args = (g_mix, w_main, w_f, f_bias, even_w_conv_a[e], fox_q_gain[e][None], fox_k_gain[e][None])

            ya, q, k, kb, v, vb, lf, st = _even_in(xp, *args, jnp.zeros((bp, conv_a - 1, d_a), F32), sp)
            lf3 = lf.reshape(bp, sp, LANES)
            c = _cumsum_time(lf3)
            c_rows = jnp.swapaxes(c[:, :, :h_b], 1, 2)[:, :, None, :]
            yb = _fox_prompt(q.reshape(bp, sp, d_b), kb.reshape(bp, sp, d_b), vb.reshape(bp, sp, d_b), c, c_rows)
            xp = _out_proj(ya, yb.reshape(bp * sp, d_b), w_out, xp)
            outs['fk_p'].append(k.reshape(bp, sp, h_b, DH))
            outs['fv_p'].append(v.reshape(bp, sp, h_b, DH))
            outs['fl_p'].append(lf3[:, :, :h_b])
            outs['ca_p'].append(st)

            ya, q, k, kb, v, vb, lf, st = _even_in(xs, *args, state_conv_a[e], ss)
            lf3 = lf.reshape(bs, ss, LANES)
            lf_all = jnp.concatenate([_pad_lanes(cache_fox_logf[e]), lf3], axis=1)
            c = _cumsum_time(lf_all)
            c_rows = jnp.swapaxes(c[:, :, :h_b], 1, 2)[:, :, None, :]
            yb = _fox_sample(q.reshape(bs, ss, d_b), cache_fox_k, cache_fox_v, e, kb.reshape(bs, ss, d_b),
                             vb.reshape(bs, ss, d_b), c, c_rows)
            xs = _out_proj(ya, yb.reshape(bs * ss, d_b), w_out, xs)
            outs['fk_s'].append(k.reshape(bs, ss, h_b, DH))
            outs['fv_s'].append(v.reshape(bs, ss, h_b, DH))
            outs['fl_s'].append(lf3[:, :, :h_b])
            outs['ca_s'].append(st)
        else:
            o = i // 2
            w_in = odd_w_in[o].astype(BF16)
            w_out = odd_w_out[o].astype(BF16)
            tril = jnp.tril(jnp.ones(sgu_w.shape[2:], dtype=bool))
            w_s = jnp.where(tril[None], sgu_w[o], 0).astype(BF16)
            b_s = sgu_b[o][:, :, None]

            def odd(x, prev, seq_len):
                lc = min(CHUNK_MLP, seq_len)
                return _odd_in(x, g_mix, w_in, odd_w_conv_c[o], conf_ln_gain[o][None], conf_ln_bias[o][None],
                               sgu_v_gain[o][None], w_s[:, :lc, :lc], b_s[:, :lc], prev, seq_len)

            yc, yd, _, st = odd(xp, jnp.zeros((bp, conv_c - 1, d_c), F32), sp)
            xp = _out_proj(yc, yd, w_out, xp)
            outs['cc_p'].append(st)
            yc, yd, vn, st = odd(xs, state_conv_c[o], ss)
            xs = _out_proj(yc, yd, w_out, xs)
            outs['cc_s'].append(st)
            outs['cv_s'].append(vn.reshape(bs, ss, -1))

        g_x = norm_xmem[i][None]
        wq = xmem_wq[i].astype(BF16)
        wo = xmem_wo[i].astype(BF16)
        q_gain = xmem_q_gain[i][None]
        mk, mv = _mem_kv(mem2, norm_mem[i][None], xmem_wk[i].astype(BF16), xmem_wv[i].astype(BF16), xmem_k_gain[i][None])
        mk3 = mk.reshape(bp, n_mem, d_x)
        mv3 = mv.reshape(bp, n_mem, d_x)
        outs['mk_p'].append(mk3.reshape(bp, n_mem, h_mem, DH))
        outs['mv_p'].append(mv3.reshape(bp, n_mem, h_mem, DH))
        xp = _mem_attn(xp, g_x, wq, q_gain, mk3, mv3, wo, sp)
        xs = _mem_attn(xs, g_x, wq, q_gain, cache_mem_k[i].reshape(bs, n_mem, d_x),
                       cache_mem_v[i].reshape(bs, n_mem, d_x), wo, ss)

        g_f = norm_ffn[i][None]
        w_rt = _pad_lanes(jnp.concatenate([moe_w_router[i], moe_w_group[i]], axis=1)).astype(BF16)
        b_rt = _pad_lanes(jnp.concatenate([moe_b_router[i], moe_b_group[i]])[None])
        xp = _moe(xp, g_f, w_rt, b_rt, moe_w_gate, moe_w_up, moe_w_down, i)
        xs = _moe(xs, g_f, w_rt, b_rt, moe_w_gate, moe_w_up, moe_w_down, i)

    st = lambda key: jnp.stack(outs[key])
    return (xp.reshape(bp, sp, d), xs.reshape(bs, ss, d),
            st('fk_p'), st('fv_p'), st('fl_p'),
            st('fk_s'), st('fv_s'), st('fl_s'),
            st('ca_p'), st('ca_s'),
            st('cc_p'), st('cc_s'),
            st('cv_s'),
            st('mk_p'), st('mv_p'))
```

```python
import functools

import jax
import jax.numpy as jnp
from jax import lax
from jax.experimental import pallas as pl
from jax.experimental.pallas import tpu as pltpu

F32 = jnp.float32
BF16 = jnp.bfloat16
I32 = jnp.int32

EPS = 1e-6
DH = 128
LANES = 128
LOG2E = 1.4426950408889634
CHUNK_MLP = 128
N_GROUPS = 4
EXP_PER_GROUP = 8
N_EXPERTS = N_GROUPS * EXP_PER_GROUP
V7X_VMEM_LIMIT = 56 * 1024 * 1024
ROW_TILE = 512
ATT_TILE = 512
CACHE_TILE = 512
CONV_ROWS, CONV_COLS = 32, 256
GATHER_TILE = 256


def _cparams(*sem):
    return pltpu.CompilerParams(dimension_semantics=sem, vmem_limit_bytes=V7X_VMEM_LIMIT)


def _rms(x, g):
    return x * lax.rsqrt(jnp.mean(x * x, axis=-1, keepdims=True) + EPS) * g


def _row_tile(n_rows, seq_len, cap=ROW_TILE):
    if seq_len >= cap:
        assert seq_len % cap == 0
        return cap
    nb = max(1, cap // seq_len)
    n_seq = n_rows // seq_len
    while n_seq % nb:
        nb -= 1
    return nb * seq_len


def _even_in_kernel(x_ref, g_ref, w_ref, wf_ref, fb_ref, wc_ref, qg_ref, kg_ref, prev_ref,
                    ya_ref, q_ref, k_ref, kb_ref, v_ref, vb_ref, lf_ref, st_ref,
                    xn_s, ab_s, ac_s, ext_s, *, nb, lt, tiles_per_seq, n_heads):
    i = pl.program_id(0)
    j = pl.program_id(1)
    halo = 8

    @pl.when(j == 0)
    def _():
        xn = _rms(x_ref[...], g_ref[...]).astype(BF16)
        xn_s[...] = xn
        z = jnp.dot(xn, wf_ref[...], preferred_element_type=F32) + fb_ref[...]
        lf_ref[...] = jnp.minimum(z, 0.0) - jnp.log1p(jnp.exp(-jnp.abs(z)))

    acc = jnp.dot(xn_s[...], w_ref[...], preferred_element_type=F32)

    @pl.when(j == 0)
    def _():
        ab_s[...] = acc

    @pl.when(j == 1)
    def _():
        ac_s[...] = acc

    @pl.when(j == 2)
    def _():
        gated = ac_s[...] * acc
        for n in range(nb):
            ext_s[n, halo:halo + lt, :] = gated[n * lt:(n + 1) * lt, :]

        @pl.when(i % tiles_per_seq == 0)
        def _():
            for n in range(nb):
                ext_s[n, halo - 2:halo, :] = prev_ref[n]

        w = wc_ref[...]
        for n in range(nb):
            conv = (w[0:1, :] * ext_s[n, halo - 2:halo - 2 + lt, :]
                    + w[1:2, :] * ext_s[n, halo - 1:halo - 1 + lt, :]
                    + w[2:3, :] * ext_s[n, halo:halo + lt, :])
            ya_ref[n * lt:(n + 1) * lt, :] = (ab_s[n * lt:(n + 1) * lt, :] * conv).astype(BF16)
            tail = ext_s[n, halo - 2 + lt:halo + lt, :]
            st_ref[n] = tail
            ext_s[n, halo - 2:halo, :] = tail

    def head_norm(gain):
        for h in range(n_heads):
            seg = acc[:, h * DH:(h + 1) * DH]
            yield h, seg * lax.rsqrt(jnp.mean(seg * seg, axis=-1, keepdims=True) + EPS) * gain

    @pl.when(j == 3)
    def _():
        for h, qn in head_norm(qg_ref[...]):
            q_ref[:, h * DH:(h + 1) * DH] = (qn * (DH ** -0.5 * LOG2E)).astype(BF16)

    @pl.when(j == 4)
    def _():
        for h, kn in head_norm(kg_ref[...]):
            k_ref[:, h * DH:(h + 1) * DH] = kn
            kb_ref[:, h * DH:(h + 1) * DH] = kn.astype(BF16)

    @pl.when(j == 5)
    def _():
        v_ref[...] = acc
        vb_ref[...] = acc.astype(BF16)


def _even_in(x, g, w_main, w_f, f_bias, w_conv, q_gain, k_gain, prev, seq_len):
    t, d = x.shape
    dc = w_conv.shape[1]
    n_col = w_main.shape[1] // dc
    assert n_col == 6
    n_heads = dc // DH
    tm = _row_tile(t, seq_len)
    lt = min(seq_len, tm)
    nb = tm // lt
    tiles_per_seq = seq_len // lt
    n_seq = t // seq_len
    kern = functools.partial(_even_in_kernel, nb=nb, lt=lt, tiles_per_seq=tiles_per_seq, n_heads=n_heads)
    row = lambda i, j: (i, 0)
    const = lambda i, j: (0, 0)
    seq3 = lambda i, j: (i // tiles_per_seq, 0, 0)
    return pl.pallas_call(
        kern,
        grid=(t // tm, n_col),
        in_specs=[
            pl.BlockSpec((tm, d), row),
            pl.BlockSpec((1, d), const),
            pl.BlockSpec((d, dc), lambda i, j: (0, j)),
            pl.BlockSpec((d, LANES), const),
            pl.BlockSpec((1, LANES), const),
            pl.BlockSpec(w_conv.shape, const),
            pl.BlockSpec((1, DH), const),
            pl.BlockSpec((1, DH), const),
            pl.BlockSpec((nb, 2, dc), seq3),
        ],
        out_specs=[
            pl.BlockSpec((tm, dc), row),
            pl.BlockSpec((tm, dc), row),
            pl.BlockSpec((tm, dc), row),
            pl.BlockSpec((tm, dc), row),
            pl.BlockSpec((tm, dc), row),
            pl.BlockSpec((tm, dc), row),
            pl.BlockSpec((tm, LANES), row),
            pl.BlockSpec((nb, 2, dc), seq3),
        ],
        out_shape=[
            jax.ShapeDtypeStruct((t, dc), BF16),
            jax.ShapeDtypeStruct((t, dc), BF16),
            jax.ShapeDtypeStruct((t, dc), F32),
            jax.ShapeDtypeStruct((t, dc), BF16),
            jax.ShapeDtypeStruct((t, dc), F32),
            jax.ShapeDtypeStruct((t, dc), BF16),
            jax.ShapeDtypeStruct((t, LANES), F32),
            jax.ShapeDtypeStruct((n_seq, 2, dc), F32),
        ],
        scratch_shapes=[
            pltpu.VMEM((tm, d), BF16),
            pltpu.VMEM((tm, dc), F32),
            pltpu.VMEM((tm, dc), F32),
            pltpu.VMEM((nb, 8 + lt, dc), F32),
        ],
        compiler_params=_cparams("arbitrary", "arbitrary"),
        name="even_in",
    )(x, g, w_main, w_f, f_bias, w_conv, q_gain, k_gain, prev)


def _cumsum_kernel(lf_ref, c_ref, carry_s, *, ts):
    @pl.when(pl.program_id(1) == 0)
    def _():
        carry_s[...] = jnp.zeros_like(carry_s)

    x = lf_ref[0]
    row = lax.broadcasted_iota(I32, x.shape, 0)
    d = 1
    while d < ts:
        x = x + jnp.where(row >= d, pltpu.roll(x, d, axis=0), 0.0)
        d *= 2
    x = x + carry_s[...]
    c_ref[0] = x * LOG2E
    carry_s[...] = x[ts - 1:ts, :]


def _time_tile(s, cap=1024):
    ts = min(s, cap)
    while s % ts or ts % 8:
        ts -= 1
    return ts


def _cumsum_time(lf):
    b, s, _ = lf.shape
    ts = _time_tile(s)
    return pl.pallas_call(
        functools.partial(_cumsum_kernel, ts=ts),
        grid=(b, s // ts),
        in_specs=[pl.BlockSpec((1, ts, LANES), lambda bi, si: (bi, si, 0))],
        out_specs=pl.BlockSpec((1, ts, LANES), lambda bi, si: (bi, si, 0)),
        out_shape=jax.ShapeDtypeStruct(lf.shape, F32),
        scratch_shapes=[pltpu.VMEM((1, LANES), F32)],
        compiler_params=_cparams("arbitrary", "arbitrary"),
        name="cumsum_time",
    )(lf)


def _online_softmax_step(s, v, m, l, acc):
    m_new = jnp.maximum(m, jnp.max(s, axis=1, keepdims=True))
    p = jnp.exp2(s - m_new)
    alpha = jnp.exp2(m - m_new)
    l = alpha * l + jnp.sum(p, axis=1, keepdims=True)
    acc = alpha * acc + jnp.dot(p.astype(BF16), v, preferred_element_type=F32)
    return m_new, l, acc


def _fox_prompt_kernel(q_ref, k_ref, v_ref, cq_ref, ck_ref, o_ref, *, tq):
    h = pl.program_id(1)
    qi = pl.program_id(2)
    q = q_ref[0]
    lane = lax.broadcasted_iota(I32, (tq, LANES), 1)
    cq = jnp.sum(jnp.where(lane == h, cq_ref[0], 0.0), axis=1, keepdims=True)

    def scores(kj):
        ks = pl.multiple_of(kj * tq, tq)
        k = k_ref[0, pl.ds(ks, tq), :]
        v = v_ref[0, pl.ds(ks, tq), :]
        s = lax.dot_general(q, k, (((1,), (1,)), ((), ())), preferred_element_type=F32)
        return s + (cq - ck_ref[0, 0, :, pl.ds(ks, tq)]), v

    def past_block(kj, carry):
        s, v = scores(kj)
        return _online_softmax_step(s, v, *carry)

    init = (jnp.full((tq, 1), -jnp.inf, F32), jnp.zeros((tq, 1), F32), jnp.zeros((tq, DH), F32))
    m, l, acc = lax.fori_loop(0, qi, past_block, init)
    s, v = scores(qi)
    r = lax.broadcasted_iota(I32, (tq, tq), 0)
    c = lax.broadcasted_iota(I32, (tq, tq), 1)
    s = jnp.where(c <= r, s, -jnp.inf)
    m, l, acc = _online_softmax_step(s, v, m, l, acc)
    o_ref[0] = (acc / l).astype(BF16)


def _fox_prompt(q, k, v, c_pad, c_rows):
    b, s, hd = q.shape
    n_heads = hd // DH
    tq = min(ATT_TILE, s)
    assert s % tq == 0
    return pl.pallas_call(
        functools.partial(_fox_prompt_kernel, tq=tq),
        grid=(b, n_heads, s // tq),
        in_specs=[
            pl.BlockSpec((1, tq, DH), lambda bi, h, qi: (bi, qi, h)),
            pl.BlockSpec((1, s, DH), lambda bi, h, qi: (bi, 0, h)),
            pl.BlockSpec((1, s, DH), lambda bi, h, qi: (bi, 0, h)),
            pl.BlockSpec((1, tq, LANES), lambda bi, h, qi: (bi, qi, 0)),
            pl.BlockSpec((1, 1, 1, s), lambda bi, h, qi: (bi, h, 0, 0)),
        ],
        out_specs=pl.BlockSpec((1, tq, DH), lambda bi, h, qi: (bi, qi, h)),
        out_shape=jax.ShapeDtypeStruct((b, s, hd), BF16),
        compiler_params=_cparams("arbitrary", "arbitrary", "arbitrary"),
        name="fox_prompt",
    )(q, k, v, c_pad, c_rows)


def _fox_sample_kernel(q_ref, ck_ref, cv_ref, kn_ref, vn_ref, cq_ref, cr_ref, o_ref,
                       m_s, l_s, acc_s, *, n_heads, t, tk, past):
    kj = pl.program_id(1)
    last = pl.num_programs(1) - 1

    @pl.when(kj == 0)
    def _():
        m_s[...] = jnp.full_like(m_s, -jnp.inf)
        l_s[...] = jnp.zeros_like(l_s)
        acc_s[...] = jnp.zeros_like(acc_s)

    lane = lax.broadcasted_iota(I32, (t, LANES), 1)
    ks = pl.multiple_of(kj * tk, tk)

    def head_q(h):
        cq = jnp.sum(jnp.where(lane == h, cq_ref[0], 0.0), axis=1, keepdims=True)
        return q_ref[0, :, h * DH:(h + 1) * DH], cq

    def softmax_step(scores, values, m, l, acc):
        s = jnp.concatenate(scores, axis=0)
        m_new = jnp.maximum(m, jnp.max(s, axis=1, keepdims=True))
        p = jnp.exp2(s - m_new)
        alpha = jnp.exp2(m - m_new)
        pb = p.astype(BF16)
        pv = jnp.concatenate([jnp.dot(pb[h * t:(h + 1) * t], values[h], preferred_element_type=F32)
                              for h in range(n_heads)], axis=0)
        return m_new, alpha * l + jnp.sum(p, axis=1, keepdims=True), alpha * acc + pv

    scores, values = [], []
    for h in range(n_heads):
        q, cq = head_q(h)
        k = ck_ref[pl.ds(h, tk, stride=n_heads), :].astype(BF16)
        s = lax.dot_general(q, k, (((1,), (1,)), ((), ())), preferred_element_type=F32)
        scores.append(s + (cq - cr_ref[0, h, :, pl.ds(ks, tk)]))
        values.append(cv_ref[pl.ds(h, tk, stride=n_heads), :].astype(BF16))
    m, l, acc = softmax_step(scores, values, m_s[...], l_s[...], acc_s[...])
    m_s[...] = m
    l_s[...] = l
    acc_s[...] = acc

    @pl.when(kj == last)
    def _():
        r = lax.broadcasted_iota(I32, (t, t), 0)
        c = lax.broadcasted_iota(I32, (t, t), 1)
        scores, values = [], []
        for h in range(n_heads):
            sl = slice(h * DH, (h + 1) * DH)
            q, cq = head_q(h)
            s = lax.dot_general(q, kn_ref[0, :, sl], (((1,), (1,)), ((), ())), preferred_element_type=F32)
            scores.append(jnp.where(c <= r, s + (cq - cr_ref[0, h, :, past:past + t]), -jnp.inf))
            values.append(vn_ref[0, :, sl])
        _, l2, acc2 = softmax_step(scores, values, m, l, acc)
        out = acc2 / l2
        for h in range(n_heads):
            o_ref[0, :, h * DH:(h + 1) * DH] = out[h * t:(h + 1) * t].astype(BF16)


def _fox_sample(q, cache_k, cache_v, layer, k_new, v_new, c_pad, c_rows):
    b, t, hd = q.shape
    n_heads = hd // DH
    past = cache_k.shape[2]
    tk = min(CACHE_TILE, past)
    assert past % tk == 0 and past % t == 0 and tk % LANES == 0
    assert cache_k.shape[1:] == (b, past, n_heads, DH)
    kern = functools.partial(_fox_sample_kernel, n_heads=n_heads, t=t, tk=tk, past=past)
    per_b = lambda bi, kj: (bi, 0, 0)
    n_kb = past // tk
    cache_rows = lambda bi, kj: ((layer * b + bi) * n_kb + kj, 0)
    return pl.pallas_call(
        kern,
        grid=(b, n_kb),
        in_specs=[
            pl.BlockSpec((1, t, hd), per_b),
            pl.BlockSpec((tk * n_heads, DH), cache_rows),
            pl.BlockSpec((tk * n_heads, DH), cache_rows),
            pl.BlockSpec((1, t, hd), per_b),
            pl.BlockSpec((1, t, hd), per_b),
            pl.BlockSpec((1, t, LANES), lambda bi, kj: (bi, past // t, 0)),
            pl.BlockSpec((1, n_heads, 1, past + t), lambda bi, kj: (bi, 0, 0, 0)),
        ],
        out_specs=pl.BlockSpec((1, t, hd), per_b),
        out_shape=jax.ShapeDtypeStruct((b, t, hd), BF16),
        scratch_shapes=[
            pltpu.VMEM((n_heads * t, 1), F32),
            pltpu.VMEM((n_heads * t, 1), F32),
            pltpu.VMEM((n_heads * t, DH), F32),
        ],
        compiler_params=_cparams("arbitrary", "arbitrary"),
        name="fox_sample",
    )(q, cache_k.reshape(-1, DH), cache_v.reshape(-1, DH), k_new, v_new, c_pad, c_rows)


def _out_proj_kernel(a_ref, b_ref, wa_ref, wb_ref, r_ref, o_ref):
    o_ref[...] = (r_ref[...]
                  + jnp.dot(a_ref[...], wa_ref[...], preferred_element_type=F32)
                  + jnp.dot(b_ref[...], wb_ref[...], preferred_element_type=F32))


def _out_proj(a, b, w, res):
    t, ka = a.shape
    kb = b.shape[1]
    d = w.shape[1]
    tm = _row_tile(t, t)
    assert ka == kb
    return pl.pallas_call(
        _out_proj_kernel,
        grid=(t // tm,),
        in_specs=[
            pl.BlockSpec((tm, ka), lambda i: (i, 0)),
            pl.BlockSpec((tm, kb), lambda i: (i, 0)),
            pl.BlockSpec((ka, d), lambda i: (0, 0)),
            pl.BlockSpec((kb, d), lambda i: (1, 0)),
            pl.BlockSpec((tm, d), lambda i: (i, 0)),
        ],
        out_specs=pl.BlockSpec((tm, d), lambda i: (i, 0)),
        out_shape=jax.ShapeDtypeStruct((t, d), F32),
        compiler_params=_cparams("arbitrary"),
        name="out_proj",
    )(a, b, w, w, res)


def _odd_in_kernel(x_ref, g_ref, w_ref, wc_ref, lg_ref, lb_ref, vg_ref, ws_ref, bs_ref, prev_ref,
                   yc_ref, yd_ref, vn_ref, st_ref,
                   xn_s, val_s, u_s, ext_s, conv_s, *, nb, lt, tiles_per_seq, taps, n_groups, lc):
    i = pl.program_id(0)
    j = pl.program_id(1)
    hist = taps - 1
    halo = ((hist + 7) // 8) * 8

    @pl.when(j == 0)
    def _():
        xn_s[...] = _rms(x_ref[...], g_ref[...]).astype(BF16)

    acc = jnp.dot(xn_s[...], w_ref[...], preferred_element_type=F32)

    @pl.when(j == 0)
    def _():
        val_s[...] = acc

    @pl.when(j == 1)
    def _():
        glu = val_s[...] * jax.nn.sigmoid(acc)
        for n in range(nb):
            ext_s[n, halo:halo + lt, :] = glu[n * lt:(n + 1) * lt, :]

        @pl.when(i % tiles_per_seq == 0)
        def _():
            for n in range(nb):
                ext_s[n, halo - hist:halo, :] = prev_ref[n]
                if halo > hist:
                    ext_s[n, 0:halo - hist, :] = jnp.zeros((halo - hist, ext_s.shape[2]), F32)

        dc = conv_s.shape[1]
        off0 = halo - hist
        win = ((off0 + hist + CONV_ROWS + 7) // 8) * 8
        for n in range(nb):
            for r0 in range(0, lt, CONV_ROWS):
                for c0 in range(0, dc, CONV_COLS):
                    cols = slice(c0, c0 + CONV_COLS)
                    blk = ext_s[n, r0:r0 + win, cols]
                    part = None
                    for b in range(8):
                        rolled = blk if b == 0 else pltpu.roll(blk, win - b, axis=0)
                        for a in range(win // 8):
                            s = 8 * a + b - off0
                            if 0 <= s < taps and 8 * a + CONV_ROWS <= win:
                                term = wc_ref[s:s + 1, cols] * rolled[8 * a:8 * a + CONV_ROWS, :]
                                part = term if part is None else part + term
                    conv_s[n * lt + r0:n * lt + r0 + CONV_ROWS, cols] = part
        for n in range(nb):
            conv = conv_s[n * lt:(n + 1) * lt, :]
            mu = jnp.mean(conv, axis=-1, keepdims=True)
            xc = conv - mu
            var = jnp.mean(xc * xc, axis=-1, keepdims=True)
            y = xc * lax.rsqrt(var + EPS) * lg_ref[...] + lb_ref[...]
            yc_ref[n * lt:(n + 1) * lt, :] = (y * jax.nn.sigmoid(y)).astype(BF16)
            tail = ext_s[n, halo - hist + lt:halo + lt, :]
            st_ref[n] = tail
            ext_s[n, halo - hist:halo, :] = tail

    @pl.when(j == 2)
    def _():
        u_s[...] = acc

    @pl.when(j == 3)
    def _():
        vn = _rms(acc, vg_ref[...])
        vn_ref[...] = vn
        vb = vn.astype(BF16)
        gw = vn.shape[1] // n_groups
        for c in range(vn.shape[0] // lc):
            rows = slice(c * lc, (c + 1) * lc)
            for g in range(n_groups):
                cols = slice(g * gw, (g + 1) * gw)
                gate = jnp.dot(ws_ref[g], vb[rows, cols], preferred_element_type=F32) + bs_ref[g]
                yd_ref[rows, cols] = (u_s[rows, cols] * gate).astype(BF16)


def _odd_in(x, g, w_in, w_conv, ln_g, ln_b, v_gain, w_s, b_s, prev, seq_len):
    t, d = x.shape
    taps, dc = w_conv.shape
    assert w_in.shape[1] == 4 * dc
    n_groups = w_s.shape[0]
    tm = _row_tile(t, seq_len, cap=ROW_TILE if seq_len >= ROW_TILE else ROW_TILE // 2)
    lt = min(seq_len, tm)
    nb = tm // lt
    tiles_per_seq = seq_len // lt
    n_seq = t // seq_len
    lc = min(CHUNK_MLP, seq_len)
    assert lt % lc == 0 and lt % CONV_ROWS == 0 and dc % CONV_COLS == 0
    hist = taps - 1
    halo = ((hist + 7) // 8) * 8
    kern = functools.partial(_odd_in_kernel, nb=nb, lt=lt, tiles_per_seq=tiles_per_seq, taps=taps,
                             n_groups=n_groups, lc=lc)
    row = lambda i, j: (i, 0)
    const = lambda i, j: (0, 0)
    const3 = lambda i, j: (0, 0, 0)
    seq3 = lambda i, j: (i // tiles_per_seq, 0, 0)
    return pl.pallas_call(
        kern,
        grid=(t // tm, 4),
        in_specs=[
            pl.BlockSpec((tm, d), row),
            pl.BlockSpec((1, d), const),
            pl.BlockSpec((d, dc), lambda i, j: (0, j)),
            pl.BlockSpec((taps, dc), const),
            pl.BlockSpec((1, dc), const),
            pl.BlockSpec((1, dc), const),
            pl.BlockSpec((1, dc), const),
            pl.BlockSpec((n_groups, lc, lc), const3),
            pl.BlockSpec((n_groups, lc, 1), const3),
            pl.BlockSpec((nb, hist, dc), seq3),
        ],
        out_specs=[
            pl.BlockSpec((tm, dc), row),
            pl.BlockSpec((tm, dc), row),
            pl.BlockSpec((tm, dc), row),
            pl.BlockSpec((nb, hist, dc), seq3),
        ],
        out_shape=[
            jax.ShapeDtypeStruct((t, dc), BF16),
            jax.ShapeDtypeStruct((t, dc), BF16),
            jax.ShapeDtypeStruct((t, dc), F32),
            jax.ShapeDtypeStruct((n_seq, hist, dc), F32),
        ],
        scratch_shapes=[
            pltpu.VMEM((tm, d), BF16),
            pltpu.VMEM((tm, dc), F32),
            pltpu.VMEM((tm, dc), F32),
            pltpu.VMEM((nb, halo + lt, dc), F32),
            pltpu.VMEM((tm, dc), F32),
        ],
        compiler_params=_cparams("arbitrary", "arbitrary"),
        name="odd_in",
    )(x, g, w_in, w_conv, ln_g, ln_b, v_gain, w_s, b_s, prev)


def _mem_kv_kernel(m_ref, g_ref, wk_ref, wv_ref, kg_ref, k_ref, v_ref, *, n_heads):
    mn = _rms(m_ref[...], g_ref[...]).astype(BF16)
    k = jnp.dot(mn, wk_ref[...], preferred_element_type=F32)
    for h in range(n_heads):
        seg = k[:, h * DH:(h + 1) * DH]
        k_ref[:, h * DH:(h + 1) * DH] = seg * lax.rsqrt(jnp.mean(seg * seg, axis=-1, keepdims=True) + EPS) * kg_ref[...]
    v_ref[...] = jnp.dot(mn, wv_ref[...], preferred_element_type=F32)


def _mem_kv(mem, g, wk, wv, k_gain):
    t, d = mem.shape
    dx = wk.shape[1]
    tm = _row_tile(t, t)
    return pl.pallas_call(
        functools.partial(_mem_kv_kernel, n_heads=dx // DH),
        grid=(t // tm,),
        in_specs=[
            pl.BlockSpec((tm, d), lambda i: (i, 0)),
            pl.BlockSpec((1, d), lambda i: (0, 0)),
            pl.BlockSpec((d, dx), lambda i: (0, 0)),
            pl.BlockSpec((d, dx), lambda i: (0, 0)),
            pl.BlockSpec((1, DH), lambda i: (0, 0)),
        ],
        out_specs=[pl.BlockSpec((tm, dx), lambda i: (i, 0))] * 2,
        out_shape=[jax.ShapeDtypeStruct((t, dx), F32)] * 2,
        compiler_params=_cparams("arbitrary"),
        name="mem_kv",
    )(mem, g, wk, wv, k_gain)


def _mem_attn_kernel(x_ref, g_ref, wq_ref, qg_ref, k_ref, v_ref, wo_ref, o_ref, att_s, *, nb, lt, n_heads):
    x = x_ref[...]
    xn = _rms(x, g_ref[...]).astype(BF16)
    q = jnp.dot(xn, wq_ref[...], preferred_element_type=F32)
    for h in range(n_heads):
        sl = slice(h * DH, (h + 1) * DH)
        seg = q[:, sl]
        qn = seg * lax.rsqrt(jnp.mean(seg * seg, axis=-1, keepdims=True) + EPS) * qg_ref[...]
        qb = (qn * (DH ** -0.5)).astype(BF16)
        for n in range(nb):
            rows = slice(n * lt, (n + 1) * lt)
            k = k_ref[n, :, sl].astype(BF16)
            v = v_ref[n, :, sl].astype(BF16)
            s = lax.dot_general(qb[rows], k, (((1,), (1,)), ((), ())), preferred_element_type=F32)
            p = jnp.exp(s - jnp.max(s, axis=1, keepdims=True))
            o = jnp.dot(p.astype(BF16), v, preferred_element_type=F32) / jnp.sum(p, axis=1, keepdims=True)
            att_s[rows, sl] = o.astype(BF16)
    o_ref[...] = x + jnp.dot(att_s[...], wo_ref[...], preferred_element_type=F32)


def _mem_attn(x, g, wq, q_gain, k, v, wo, seq_len):
    t, d = x.shape
    dx = wq.shape[1]
    n_mem = k.shape[1]
    tm = _row_tile(t, seq_len, cap=256 if seq_len < 256 else ROW_TILE)
    lt = min(seq_len, tm)
    nb = tm // lt
    tiles_per_seq = seq_len // lt
    kern = functools.partial(_mem_attn_kernel, nb=nb, lt=lt, n_heads=dx // DH)
    return pl.pallas_call(
        kern,
        grid=(t // tm,),
        in_specs=[
            pl.BlockSpec((tm, d), lambda i: (i, 0)),
            pl.BlockSpec((1, d), lambda i: (0, 0)),
            pl.BlockSpec((d, dx), lambda i: (0, 0)),
            pl.BlockSpec((1, DH), lambda i: (0, 0)),
            pl.BlockSpec((nb, n_mem, dx), lambda i: (i // tiles_per_seq, 0, 0)),
            pl.BlockSpec((nb, n_mem, dx), lambda i: (i // tiles_per_seq, 0, 0)),
            pl.BlockSpec((dx, d), lambda i: (0, 0)),
        ],
        out_specs=pl.BlockSpec((tm, d), lambda i: (i, 0)),
        out_shape=jax.ShapeDtypeStruct((t, d), F32),
        scratch_shapes=[pltpu.VMEM((tm, dx), BF16)],
        compiler_params=_cparams("arbitrary"),
        name="mem_attn",
    )(x, g, wq, q_gain, k, v, wo)


GROUP_LANE0 = N_EXPERTS
PAIRS_PER_GROUP = EXP_PER_GROUP * (EXP_PER_GROUP - 1) // 2
N_CLASSES = N_GROUPS * PAIRS_PER_GROUP
META_C, META_R, META_G = 0, 1, 2
MOE_BLOCK = 128


def _router_kernel(x_ref, g_ref, w_ref, b_ref, meta_ref, cnt_ref, *, tm):
    @pl.when(pl.program_id(0) == 0)
    def _():
        cnt_ref[...] = jnp.zeros_like(cnt_ref)

    xn = _rms(x_ref[...], g_ref[...]).astype(BF16)
    logits = jnp.dot(xn, w_ref[...], preferred_element_type=F32) + b_ref[...]
    lane = lax.broadcasted_iota(I32, (tm, LANES), 1)
    big = jnp.int32(LANES)

    def first_argmax(vals):
        top = jnp.max(vals, axis=1, keepdims=True)
        return top, jnp.min(jnp.where(vals == top, lane, big), axis=1, keepdims=True)

    is_grp = (lane >= GROUP_LANE0) & (lane < GROUP_LANE0 + N_GROUPS)
    lg = jnp.where(is_grp, logits, -jnp.inf)
    g_top, g_lane = first_argmax(lg)
    p_sel = 1.0 / jnp.sum(jnp.where(is_grp, jnp.exp(lg - g_top), 0.0), axis=1, keepdims=True)
    e0 = (g_lane - GROUP_LANE0) * EXP_PER_GROUP
    le = jnp.where((lane >= e0) & (lane < e0 + EXP_PER_GROUP), logits, -jnp.inf)
    v1, i1 = first_argmax(le)
    v2, i2 = first_argmax(jnp.where(lane == i1, -jnp.inf, le))
    e = jnp.exp(v2 - v1)
    g1 = p_sel / (1.0 + e)
    g2 = p_sel * e / (1.0 + e)

    swap = i2 < i1
    lo = jnp.minimum(i1, i2) - e0
    hi = jnp.maximum(i1, i2) - e0
    pair = jnp.right_shift(lo * (2 * EXP_PER_GROUP - 1 - lo), 1) + (hi - lo - 1)
    cls = (g_lane - GROUP_LANE0) * PAIRS_PER_GROUP + pair
    g_lo = jnp.where(swap, g2, g1)
    g_hi = jnp.where(swap, g1, g2)

    sel = lane == cls
    onehot = jnp.where(sel, 1.0, 0.0)
    r = lax.broadcasted_iota(I32, (tm, tm), 0)
    c = lax.broadcasted_iota(I32, (tm, tm), 1)
    earlier = jnp.where(c < r, 1.0, 0.0).astype(BF16)
    before = jnp.dot(earlier, onehot.astype(BF16), preferred_element_type=F32) + cnt_ref[...]
    rank = jnp.sum(jnp.where(sel, before, 0.0), axis=1, keepdims=True)
    cnt_ref[...] = cnt_ref[...] + jnp.sum(onehot, axis=0, keepdims=True)

    meta = jnp.zeros((tm, LANES), F32)
    for k, val in ((META_C, cls.astype(F32)), (META_R, rank), (META_G, g_lo), (META_G + 1, g_hi)):
        meta = jnp.where(lane == k, val, meta)
    meta_ref[...] = meta


def _router(x, g, w_rt, b_rt):
    t, d = x.shape
    tm = _row_tile(t, t)
    return pl.pallas_call(
        functools.partial(_router_kernel, tm=tm),
        grid=(t // tm,),
        in_specs=[
            pl.BlockSpec((tm, d), lambda i: (i, 0)),
            pl.BlockSpec((1, d), lambda i: (0, 0)),
            pl.BlockSpec((d, LANES), lambda i: (0, 0)),
            pl.BlockSpec((1, LANES), lambda i: (0, 0)),
        ],
        out_specs=[pl.BlockSpec((tm, LANES), lambda i: (i, 0)), pl.BlockSpec((1, LANES), lambda i: (0, 0))],
        out_shape=[jax.ShapeDtypeStruct((t, LANES), F32), jax.ShapeDtypeStruct((1, LANES), F32)],
        compiler_params=_cparams("arbitrary"),
        name="moe_router",
    )(x, g, w_rt, b_rt)


def _row_copy(src, src_row, dst, dst_row, sem):
    return pltpu.make_async_copy(src.at[pl.ds(src_row, 1)], dst.at[pl.ds(dst_row, 1)], sem)


def _dispatch_kernel(pos_ref, x_ref, g_ref, meta_ref, zero_ref, xb_ref, xn_s, sem, *, tm, d):
    del zero_ref
    i = pl.program_id(0)
    n_steps = pl.num_programs(0)
    slot = i % 2
    base = i * tm

    def drain(sl):
        def wait_row(r, _):
            _row_copy(xn_s.at[sl], 0, xb_ref, 0, sem.at[sl]).wait()
            return 0

        lax.fori_loop(0, tm, wait_row, 0)

    @pl.when(i >= 2)
    def _():
        drain(slot)

    xn_s[slot, :, :d] = _rms(x_ref[...], g_ref[...])
    xn_s[slot, :, d:] = meta_ref[...]

    def issue(r, _):
        _row_copy(xn_s.at[slot], r, xb_ref, pos_ref[base + r], sem.at[slot]).start()
        return 0

    lax.fori_loop(0, tm, issue, 0)

    @pl.when(i == n_steps - 1)
    def _():
        @pl.when(n_steps >= 2)
        def _():
            drain(1 - slot)

        drain(slot)


def _dispatch(pos, x, g, meta, n_rows):
    t, d = x.shape
    tm = min(GATHER_TILE, t)
    assert t % tm == 0
    zeros = jnp.zeros((n_rows, d + LANES), F32)
    return pl.pallas_call(
        functools.partial(_dispatch_kernel, tm=tm, d=d),
        grid_spec=pltpu.PrefetchScalarGridSpec(
            num_scalar_prefetch=1,
            grid=(t // tm,),
            in_specs=[
                pl.BlockSpec((tm, d), lambda i, pos: (i, 0)),
                pl.BlockSpec((1, d), lambda i, pos: (0, 0)),
                pl.BlockSpec((tm, LANES), lambda i, pos: (i, 0)),
                pl.BlockSpec(memory_space=pl.ANY),
            ],
            out_specs=pl.BlockSpec(memory_space=pl.ANY),
            scratch_shapes=[pltpu.VMEM((2, tm, d + LANES), F32), pltpu.SemaphoreType.DMA((2,))],
        ),
        out_shape=jax.ShapeDtypeStruct((n_rows, d + LANES), F32),
        input_output_aliases={4: 0},
        compiler_params=_cparams("arbitrary"),
        name="moe_dispatch",
    )(pos, x, g, meta, zeros)


def _expert_kernel(ba_ref, bb_ref, nu_ref, x_ref, wga_ref, wua_ref, wda_ref, wgb_ref, wub_ref, wdb_ref,
                   y_ref, *, d):
    del ba_ref, bb_ref
    used = pl.program_id(0) < nu_ref[0]

    @pl.when(used)
    def _():
        x = x_ref[:, :d].astype(BF16)
        rec = x_ref[:, d:]

        def ffn(wg_ref, wu_ref, wd_ref):
            gate = jnp.dot(x, wg_ref[0, 0], preferred_element_type=F32)
            up = jnp.dot(x, wu_ref[0, 0], preferred_element_type=F32)
            hid = (gate * jax.nn.sigmoid(gate) * up).astype(BF16)
            return jnp.dot(hid, wd_ref[0, 0], preferred_element_type=F32)

        y_ref[...] = (rec[:, META_G:META_G + 1] * ffn(wga_ref, wua_ref, wda_ref)
                      + rec[:, META_G + 1:META_G + 2] * ffn(wgb_ref, wub_ref, wdb_ref))

    @pl.when(jnp.logical_not(used))
    def _():
        y_ref[...] = jnp.zeros_like(y_ref)


def _experts(blk_a, blk_b, n_used, xb, w_gate, w_up, w_down, layer, bm):
    de, d = w_down.shape[2:]
    p = xb.shape[0]
    n_blocks = p // bm

    def xmap(b, ba, bb, nu):
        return (jnp.minimum(b, jnp.maximum(nu[0] - 1, 0)), 0)

    def amap(b, ba, bb, nu):
        return (layer, ba[b], 0, 0)

    def bmap(b, ba, bb, nu):
        return (layer, bb[b], 0, 0)

    return pl.pallas_call(
        functools.partial(_expert_kernel, d=d),
        grid_spec=pltpu.PrefetchScalarGridSpec(
            num_scalar_prefetch=3,
            grid=(n_blocks,),
            in_specs=[
                pl.BlockSpec((bm, d + LANES), xmap),
                pl.BlockSpec((1, 1, d, de), amap),
                pl.BlockSpec((1, 1, d, de), amap),
                pl.BlockSpec((1, 1, de, d), amap),
                pl.BlockSpec((1, 1, d, de), bmap),
                pl.BlockSpec((1, 1, d, de), bmap),
                pl.BlockSpec((1, 1, de, d), bmap),
            ],
            out_specs=pl.BlockSpec((bm, d), lambda b, ba, bb, nu: (b, 0)),
        ),
        out_shape=jax.ShapeDtypeStruct((p, d), F32),
        compiler_params=_cparams("arbitrary"),
        name="moe_experts",
    )(blk_a, blk_b, n_used, xb, w_gate, w_up, w_down, w_gate, w_up, w_down)


def _combine_kernel(pos_ref, x_ref, yb_ref, o_ref, buf_s, sem, *, tm):
    i = pl.program_id(0)
    slot = i % 2

    def issue(tile, sl):
        base = tile * tm

        def start_row(r, _):
            _row_copy(yb_ref, pos_ref[base + r], buf_s.at[sl], r, sem.at[sl]).start()
            return 0

        lax.fori_loop(0, tm, start_row, 0)

    @pl.when(i == 0)
    def _():
        issue(0, 0)

    @pl.when(i + 1 < pl.num_programs(0))
    def _():
        issue(i + 1, 1 - slot)

    def wait_row(r, _):
        _row_copy(yb_ref, 0, buf_s.at[slot], 0, sem.at[slot]).wait()
        return 0

    lax.fori_loop(0, tm, wait_row, 0)
    o_ref[...] = x_ref[...] + buf_s[slot]


def _combine(pos, x, yb):
    t, d = x.shape
    tm = min(GATHER_TILE, t)
    return pl.pallas_call(
        functools.partial(_combine_kernel, tm=tm),
        grid_spec=pltpu.PrefetchScalarGridSpec(
            num_scalar_prefetch=1,
            grid=(t // tm,),
            in_specs=[
                pl.BlockSpec((tm, d), lambda i, pos: (i, 0)),
                pl.BlockSpec(memory_space=pl.ANY),
            ],
            out_specs=pl.BlockSpec((tm, d), lambda i, pos: (i, 0)),
            scratch_shapes=[pltpu.VMEM((2, tm, d), F32), pltpu.SemaphoreType.DMA((2,))],
        ),
        out_shape=jax.ShapeDtypeStruct((t, d), F32),
        compiler_params=_cparams("arbitrary"),
        name="moe_combine",
    )(pos, x, yb)


def _moe(x, g, w_rt, b_rt, w_gate, w_up, w_down, layer):
    t, d = x.shape
    meta, cnt = _router(x, g, w_rt, b_rt)
    bm = MOE_BLOCK
    counts = cnt[0, :N_CLASSES].astype(I32)
    padded = ((counts + bm - 1) // bm) * bm
    pend = jnp.cumsum(padded)
    pstart = pend - padded
    n_blocks = (t + N_CLASSES * (bm - 1) + bm - 1) // bm
    cls = meta[:, META_C].astype(I32)
    rank = meta[:, META_R].astype(I32)
    onehot = cls[:, None] == jnp.arange(N_CLASSES, dtype=I32)[None, :]
    pos = jnp.sum(jnp.where(onehot, pstart[None, :], 0), axis=-1) + rank
    blk_row0 = jnp.arange(n_blocks, dtype=I32) * bm
    blk_cls = jnp.minimum(jnp.sum((pend[None, :] <= blk_row0[:, None]).astype(I32), axis=1), N_CLASSES - 1)
    pairs = [(grp * EXP_PER_GROUP + a, grp * EXP_PER_GROUP + b) for grp in range(N_GROUPS)
             for a in range(EXP_PER_GROUP) for b in range(a + 1, EXP_PER_GROUP)]
    blk_a = jnp.asarray([p[0] for p in pairs], I32)[blk_cls]
    blk_b = jnp.asarray([p[1] for p in pairs], I32)[blk_cls]
    n_used = (pend[-1:] // bm).astype(I32)
    xb = _dispatch(pos, x, g, meta, n_blocks * bm)
    yb = _experts(blk_a, blk_b, n_used, xb, w_gate, w_up, w_down, layer, bm)
    return _combine(pos, x, yb)


def _pad_lanes(a):
    return jnp.pad(a, [(0, 0)] * (a.ndim - 1) + [(0, LANES - a.shape[-1])])


def kernel(x_prompt, x_sample, mem_prompt, cache_fox_k, cache_fox_v, cache_fox_logf, state_conv_a, state_conv_c, cache_mem_k, cache_mem_v, norm_mix, norm_xmem, norm_mem, norm_ffn, even_w_in, even_w_conv_a, fox_q_gain, fox_k_gain, fox_f_bias, even_w_out, odd_w_in, odd_w_conv_c, conf_ln_gain, conf_ln_bias, sgu_v_gain, sgu_w, sgu_b, odd_w_out, xmem_wq, xmem_wk, xmem_wv, xmem_q_gain, xmem_k_gain, xmem_wo, moe_w_group, moe_b_group, moe_w_router, moe_b_router, moe_w_gate, moe_w_up, moe_w_down):
    bp, sp, d = x_prompt.shape
    bs, ss, _ = x_sample.shape
    depth = norm_mix.shape[0]
    n_mem = mem_prompt.shape[1]
    d_a = even_w_conv_a.shape[2]
    conv_a = even_w_conv_a.shape[1]
    conv_c = odd_w_conv_c.shape[1]
    d_c = odd_w_conv_c.shape[2]
    h_b = fox_f_bias.shape[1]
    d_b = h_b * DH
    d_x = xmem_wq.shape[2]
    h_mem = d_x // DH
    n_main = 3 * d_a + 3 * d_b

    xp = x_prompt.reshape(bp * sp, d)
    xs = x_sample.reshape(bs * ss, d)
    mem2 = mem_prompt.reshape(bp * n_mem, d)
    moe_w = (moe_w_gate.astype(BF16), moe_w_up.astype(BF16), moe_w_down.astype(BF16))
    outs = {k: [] for k in ('fk_p', 'fv_p', 'fl_p', 'fk_s', 'fv_s', 'fl_s', 'ca_p', 'ca_s', 'cc_p', 'cc_s', 'cv_s', 'mk_p', 'mv_p')}

    for i in range(depth):
        g_mix = norm_mix[i][None]
        if i % 2 == 0:
            e = i // 2
            w_in = even_w_in[e]
            w_main = w_in[:, :n_main].astype(BF16)
            w_f = _pad_lanes(w_in[:, n_main:]).astype(BF16)
            f_bias = _pad_lanes(fox_f_bias[e][None])
            w_out = even_w_out[e].astype(BF16)
            args = (g_mix, w_main, w_f, f_bias, even_w_conv_a[e], fox_q_gain[e][None], fox_k_gain[e][None])

            ya, q, k, kb, v, vb, lf, st = _even_in(xp, *args, jnp.zeros((bp, conv_a - 1, d_a), F32), sp)
            lf3 = lf.reshape(bp, sp, LANES)
            c = _cumsum_time(lf3)
            c_rows = jnp.swapaxes(c[:, :, :h_b], 1, 2)[:, :, None, :]
            yb = _fox_prompt(q.reshape(bp, sp, d_b), kb.reshape(bp, sp, d_b), vb.reshape(bp, sp, d_b), c, c_rows)
            xp = _out_proj(ya, yb.reshape(bp * sp, d_b), w_out, xp)
            outs['fk_p'].append(k.reshape(bp, sp, h_b, DH))
            outs['fv_p'].append(v.reshape(bp, sp, h_b, DH))
            outs['fl_p'].append(lf3[:, :, :h_b])
            outs['ca_p'].append(st)

            ya, q, k, kb, v, vb, lf, st = _even_in(xs, *args, state_conv_a[e], ss)
            lf3 = lf.reshape(bs, ss, LANES)
            lf_all = jnp.concatenate([_pad_lanes(cache_fox_logf[e]), lf3], axis=1)
            c = _cumsum_time(lf_all)
            c_rows = jnp.swapaxes(c[:, :, :h_b], 1, 2)[:, :, None, :]
            yb = _fox_sample(q.reshape(bs, ss, d_b), cache_fox_k, cache_fox_v, e, kb.reshape(bs, ss, d_b),
                             vb.reshape(bs, ss, d_b), c, c_rows)
            xs = _out_proj(ya, yb.reshape(bs * ss, d_b), w_out, xs)
            outs['fk_s'].append(k.reshape(bs, ss, h_b, DH))
            outs['fv_s'].append(v.reshape(bs, ss, h_b, DH))
            outs['fl_s'].append(lf3[:, :, :h_b])
            outs['ca_s'].append(st)
        else:
            o = i // 2
            w_in = odd_w_in[o].astype(BF16)
            w_out = odd_w_out[o].astype(BF16)
            tril = jnp.tril(jnp.ones(sgu_w.shape[2:], dtype=bool))
            w_s = jnp.where(tril[None], sgu_w[o], 0).astype(BF16)
            b_s = sgu_b[o][:, :, None]

            def odd(x, prev, seq_len):
                lc = min(CHUNK_MLP, seq_len)
                return _odd_in(x, g_mix, w_in, odd_w_conv_c[o], conf_ln_gain[o][None], conf_ln_bias[o][None],
                               sgu_v_gain[o][None], w_s[:, :lc, :lc], b_s[:, :lc], prev, seq_len)

            yc, yd, _, st = odd(xp, jnp.zeros((bp, conv_c - 1, d_c), F32), sp)
            xp = _out_proj(yc, yd, w_out, xp)
            outs['cc_p'].append(st)
            yc, yd, vn, st = odd(xs, state_conv_c[o], ss)
            xs = _out_proj(yc, yd, w_out, xs)
            outs['cc_s'].append(st)
            outs['cv_s'].append(vn.reshape(bs, ss, -1))

        g_x = norm_xmem[i][None]
        wq = xmem_wq[i].astype(BF16)
        wo = xmem_wo[i].astype(BF16)
        q_gain = xmem_q_gain[i][None]
        mk, mv = _mem_kv(mem2, norm_mem[i][None], xmem_wk[i].astype(BF16), xmem_wv[i].astype(BF16), xmem_k_gain[i][None])
        mk3 = mk.reshape(bp, n_mem, d_x)
        mv3 = mv.reshape(bp, n_mem, d_x)
        outs['mk_p'].append(mk3.reshape(bp, n_mem, h_mem, DH))
        outs['mv_p'].append(mv3.reshape(bp, n_mem, h_mem, DH))
        xp = _mem_attn(xp, g_x, wq, q_gain, mk3, mv3, wo, sp)
        xs = _mem_attn(xs, g_x, wq, q_gain, cache_mem_k[i].reshape(bs, n_mem, d_x),
                       cache_mem_v[i].reshape(bs, n_mem, d_x), wo, ss)

        g_f = norm_ffn[i][None]
        w_rt = _pad_lanes(jnp.concatenate([moe_w_router[i], moe_w_group[i]], axis=1)).astype(BF16)
        b_rt = _pad_lanes(jnp.concatenate([moe_b_router[i], moe_b_group[i]])[None])
        xp = _moe(xp, g_f, w_rt, b_rt, *moe_w, i)
        xs = _moe(xs, g_f, w_rt, b_rt, *moe_w, i)

    st = lambda key: jnp.stack(outs[key])
    return (xp.reshape(bp, sp, d), xs.reshape(bs, ss, d),
            st('fk_p'), st('fv_p'), st('fl_p'),
            st('fk_s'), st('fv_s'), st('fl_s'),
            st('ca_p'), st('ca_s'),
            st('cc_p'), st('cc_s'),
            st('cv_s'),
            st('mk_p'), st('mv_p'))
```

```python
import functools
from typing import NamedTuple

import jax
import jax.numpy as jnp
from jax import lax
from jax.experimental import pallas as pl
from jax.experimental.pallas import tpu as pltpu

F32 = jnp.float32
BF16 = jnp.bfloat16
I32 = jnp.int32

EPS = 1e-6
DH = 128
LANES = 128
LOG2E = 1.4426950408889634
CHUNK_MLP = 128
N_GROUPS = 4
EXP_PER_GROUP = 8
N_EXPERTS = N_GROUPS * EXP_PER_GROUP
V7X_VMEM_LIMIT = 56 * 1024 * 1024
ROW_TILE = 512
ATT_TILE = 512
CACHE_TILE = 512
CONV_ROWS, CONV_COLS = 32, 256
COPY_UNROLL = 8
GATHER_TILE = 256


def _cparams(*sem):
    return pltpu.CompilerParams(dimension_semantics=sem, vmem_limit_bytes=V7X_VMEM_LIMIT)


def _rms(x, g):
    return x * lax.rsqrt(jnp.mean(x * x, axis=-1, keepdims=True) + EPS) * g


class _Rows(NamedTuple):
    arr: jax.Array
    row0: int
    n: int


def _view(x):
    return x if isinstance(x, _Rows) else _Rows(x, 0, x.shape[0])


def _row_tile(n_rows, seq_len, cap=ROW_TILE):
    if seq_len >= cap:
        assert seq_len % cap == 0
        return cap
    nb = max(1, cap // seq_len)
    n_seq = n_rows // seq_len
    while n_seq % nb:
        nb -= 1
    return nb * seq_len


def _even_in_kernel(x_ref, g_ref, w_ref, wf_ref, fb_ref, wc_ref, qg_ref, kg_ref, prev_ref,
                    ya_ref, q_ref, k_ref, kb_ref, v_ref, vb_ref, lf_ref, st_ref,
                    xn_s, ab_s, ac_s, ext_s, *, nb, lt, tiles_per_seq, n_heads):
    i = pl.program_id(0)
    j = pl.program_id(1)
    halo = 8

    @pl.when(j == 0)
    def _():
        xn = _rms(x_ref[...], g_ref[...]).astype(BF16)
        xn_s[...] = xn
        z = jnp.dot(xn, wf_ref[...], preferred_element_type=F32) + fb_ref[...]
        lf_ref[...] = jnp.minimum(z, 0.0) - jnp.log1p(jnp.exp(-jnp.abs(z)))

    acc = jnp.dot(xn_s[...], w_ref[...], preferred_element_type=F32)

    @pl.when(j == 0)
    def _():
        ab_s[...] = acc

    @pl.when(j == 1)
    def _():
        ac_s[...] = acc

    @pl.when(j == 2)
    def _():
        gated = ac_s[...] * acc
        for n in range(nb):
            ext_s[n, halo:halo + lt, :] = gated[n * lt:(n + 1) * lt, :]

        @pl.when(i % tiles_per_seq == 0)
        def _():
            for n in range(nb):
                ext_s[n, halo - 2:halo, :] = prev_ref[n]

        w = wc_ref[...]
        for n in range(nb):
            conv = (w[0:1, :] * ext_s[n, halo - 2:halo - 2 + lt, :]
                    + w[1:2, :] * ext_s[n, halo - 1:halo - 1 + lt, :]
                    + w[2:3, :] * ext_s[n, halo:halo + lt, :])
            ya_ref[n * lt:(n + 1) * lt, :] = (ab_s[n * lt:(n + 1) * lt, :] * conv).astype(BF16)
            tail = ext_s[n, halo - 2 + lt:halo + lt, :]
            st_ref[n] = tail
            ext_s[n, halo - 2:halo, :] = tail

    def head_norm(gain):
        for h in range(n_heads):
            seg = acc[:, h * DH:(h + 1) * DH]
            yield h, seg * lax.rsqrt(jnp.mean(seg * seg, axis=-1, keepdims=True) + EPS) * gain

    @pl.when(j == 3)
    def _():
        for h, qn in head_norm(qg_ref[...]):
            q_ref[:, h * DH:(h + 1) * DH] = (qn * (DH ** -0.5 * LOG2E)).astype(BF16)

    @pl.when(j == 4)
    def _():
        for h, kn in head_norm(kg_ref[...]):
            k_ref[:, h * DH:(h + 1) * DH] = kn
            kb_ref[:, h * DH:(h + 1) * DH] = kn.astype(BF16)

    @pl.when(j == 5)
    def _():
        v_ref[...] = acc
        vb_ref[...] = acc.astype(BF16)


def _even_in(x, g, w_main, w_f, f_bias, w_conv, q_gain, k_gain, prev, seq_len):
    x = _view(x)
    t, d = x.n, x.arr.shape[1]
    dc = w_conv.shape[1]
    n_col = w_main.shape[1] // dc
    assert n_col == 6
    n_heads = dc // DH
    tm = _row_tile(t, seq_len)
    lt = min(seq_len, tm)
    nb = tm // lt
    tiles_per_seq = seq_len // lt
    n_seq = t // seq_len
    kern = functools.partial(_even_in_kernel, nb=nb, lt=lt, tiles_per_seq=tiles_per_seq, n_heads=n_heads)
    row = lambda i, j: (i, 0)
    const = lambda i, j: (0, 0)
    seq3 = lambda i, j: (i // tiles_per_seq, 0, 0)
    return pl.pallas_call(
        kern,
        grid=(t // tm, n_col),
        in_specs=[
            pl.BlockSpec((tm, d), lambda i, j: (i + x.row0 // tm, 0)),
            pl.BlockSpec((1, d), const),
            pl.BlockSpec((d, dc), lambda i, j: (0, j)),
            pl.BlockSpec((d, LANES), const),
            pl.BlockSpec((1, LANES), const),
            pl.BlockSpec(w_conv.shape, const),
            pl.BlockSpec((1, DH), const),
            pl.BlockSpec((1, DH), const),
            pl.BlockSpec((nb, 2, dc), seq3),
        ],
        out_specs=[
            pl.BlockSpec((tm, dc), row),
            pl.BlockSpec((tm, dc), row),
            pl.BlockSpec((tm, dc), row),
            pl.BlockSpec((tm, dc), row),
            pl.BlockSpec((tm, dc), row),
            pl.BlockSpec((tm, dc), row),
            pl.BlockSpec((tm, LANES), row),
            pl.BlockSpec((nb, 2, dc), seq3),
        ],
        out_shape=[
            jax.ShapeDtypeStruct((t, dc), BF16),
            jax.ShapeDtypeStruct((t, dc), BF16),
            jax.ShapeDtypeStruct((t, dc), F32),
            jax.ShapeDtypeStruct((t, dc), BF16),
            jax.ShapeDtypeStruct((t, dc), F32),
            jax.ShapeDtypeStruct((t, dc), BF16),
            jax.ShapeDtypeStruct((t, LANES), F32),
            jax.ShapeDtypeStruct((n_seq, 2, dc), F32),
        ],
        scratch_shapes=[
            pltpu.VMEM((tm, d), BF16),
            pltpu.VMEM((tm, dc), F32),
            pltpu.VMEM((tm, dc), F32),
            pltpu.VMEM((nb, 8 + lt, dc), F32),
        ],
        compiler_params=_cparams("arbitrary", "arbitrary"),
        name="even_in",
    )(x.arr, g, w_main, w_f, f_bias, w_conv, q_gain, k_gain, prev)


def _cumsum_kernel(lf_ref, c_ref, carry_s, *, ts):
    @pl.when(pl.program_id(1) == 0)
    def _():
        carry_s[...] = jnp.zeros_like(carry_s)

    x = lf_ref[0]
    row = lax.broadcasted_iota(I32, x.shape, 0)
    d = 1
    while d < ts:
        x = x + jnp.where(row >= d, pltpu.roll(x, d, axis=0), 0.0)
        d *= 2
    x = x + carry_s[...]
    c_ref[0] = x * LOG2E
    carry_s[...] = x[ts - 1:ts, :]


def _time_tile(s, cap=1024):
    ts = min(s, cap)
    while s % ts or ts % 8:
        ts -= 1
    return ts


def _cumsum_time(lf):
    b, s, _ = lf.shape
    ts = _time_tile(s)
    return pl.pallas_call(
        functools.partial(_cumsum_kernel, ts=ts),
        grid=(b, s // ts),
        in_specs=[pl.BlockSpec((1, ts, LANES), lambda bi, si: (bi, si, 0))],
        out_specs=pl.BlockSpec((1, ts, LANES), lambda bi, si: (bi, si, 0)),
        out_shape=jax.ShapeDtypeStruct(lf.shape, F32),
        scratch_shapes=[pltpu.VMEM((1, LANES), F32)],
        compiler_params=_cparams("arbitrary", "arbitrary"),
        name="cumsum_time",
    )(lf)


def _online_softmax_step(s, v, m, l, acc):
    m_new = jnp.maximum(m, jnp.max(s, axis=1, keepdims=True))
    p = jnp.exp2(s - m_new)
    alpha = jnp.exp2(m - m_new)
    l = alpha * l + jnp.sum(p, axis=1, keepdims=True)
    acc = alpha * acc + jnp.dot(p.astype(BF16), v, preferred_element_type=F32)
    return m_new, l, acc


def _fox_prompt_kernel(q_ref, k_ref, v_ref, cq_ref, ck_ref, o_ref, *, tq):
    h = pl.program_id(1)
    qi = pl.program_id(2)
    q = q_ref[0]
    lane = lax.broadcasted_iota(I32, (tq, LANES), 1)
    cq = jnp.sum(jnp.where(lane == h, cq_ref[0], 0.0), axis=1, keepdims=True)

    def scores(kj):
        ks = pl.multiple_of(kj * tq, tq)
        k = k_ref[0, pl.ds(ks, tq), :]
        v = v_ref[0, pl.ds(ks, tq), :]
        s = lax.dot_general(q, k, (((1,), (1,)), ((), ())), preferred_element_type=F32)
        return s + (cq - ck_ref[0, 0, :, pl.ds(ks, tq)]), v

    def past_block(kj, carry):
        s, v = scores(kj)
        return _online_softmax_step(s, v, *carry)

    init = (jnp.full((tq, 1), -jnp.inf, F32), jnp.zeros((tq, 1), F32), jnp.zeros((tq, DH), F32))
    m, l, acc = lax.fori_loop(0, qi, past_block, init)
    s, v = scores(qi)
    r = lax.broadcasted_iota(I32, (tq, tq), 0)
    c = lax.broadcasted_iota(I32, (tq, tq), 1)
    s = jnp.where(c <= r, s, -jnp.inf)
    m, l, acc = _online_softmax_step(s, v, m, l, acc)
    o_ref[0] = (acc / l).astype(BF16)


def _fox_prompt(q, k, v, c_pad, c_rows):
    b, s, hd = q.shape
    n_heads = hd // DH
    tq = min(ATT_TILE, s)
    assert s % tq == 0
    return pl.pallas_call(
        functools.partial(_fox_prompt_kernel, tq=tq),
        grid=(b, n_heads, s // tq),
        in_specs=[
            pl.BlockSpec((1, tq, DH), lambda bi, h, qi: (bi, qi, h)),
            pl.BlockSpec((1, s, DH), lambda bi, h, qi: (bi, 0, h)),
            pl.BlockSpec((1, s, DH), lambda bi, h, qi: (bi, 0, h)),
            pl.BlockSpec((1, tq, LANES), lambda bi, h, qi: (bi, qi, 0)),
            pl.BlockSpec((1, 1, 1, s), lambda bi, h, qi: (bi, h, 0, 0)),
        ],
        out_specs=pl.BlockSpec((1, tq, DH), lambda bi, h, qi: (bi, qi, h)),
        out_shape=jax.ShapeDtypeStruct((b, s, hd), BF16),
        compiler_params=_cparams("arbitrary", "arbitrary", "arbitrary"),
        name="fox_prompt",
    )(q, k, v, c_pad, c_rows)


def _fox_sample_kernel(q_ref, ck_ref, cv_ref, kn_ref, vn_ref, cq_ref, cr_ref, o_ref,
                       m_s, l_s, acc_s, *, n_heads, t, tk, past):
    kj = pl.program_id(1)
    last = pl.num_programs(1) - 1

    @pl.when(kj == 0)
    def _():
        m_s[...] = jnp.full_like(m_s, -jnp.inf)
        l_s[...] = jnp.zeros_like(l_s)
        acc_s[...] = jnp.zeros_like(acc_s)

    lane = lax.broadcasted_iota(I32, (t, LANES), 1)
    ks = pl.multiple_of(kj * tk, tk)

    def head_q(h):
        cq = jnp.sum(jnp.where(lane == h, cq_ref[0], 0.0), axis=1, keepdims=True)
        return q_ref[0, :, h * DH:(h + 1) * DH], cq

    def softmax_step(scores, values, m, l, acc):
        s = jnp.concatenate(scores, axis=0)
        m_new = jnp.maximum(m, jnp.max(s, axis=1, keepdims=True))
        p = jnp.exp2(s - m_new)
        alpha = jnp.exp2(m - m_new)
        pb = p.astype(BF16)
        pv = jnp.concatenate([jnp.dot(pb[h * t:(h + 1) * t], values[h], preferred_element_type=F32)
                              for h in range(n_heads)], axis=0)
        return m_new, alpha * l + jnp.sum(p, axis=1, keepdims=True), alpha * acc + pv

    scores, values = [], []
    for h in range(n_heads):
        q, cq = head_q(h)
        k = ck_ref[pl.ds(h, tk, stride=n_heads), :].astype(BF16)
        s = lax.dot_general(q, k, (((1,), (1,)), ((), ())), preferred_element_type=F32)
        scores.append(s + (cq - cr_ref[0, h, :, pl.ds(ks, tk)]))
        values.append(cv_ref[pl.ds(h, tk, stride=n_heads), :].astype(BF16))
    m, l, acc = softmax_step(scores, values, m_s[...], l_s[...], acc_s[...])
    m_s[...] = m
    l_s[...] = l
    acc_s[...] = acc

    @pl.when(kj == last)
    def _():
        r = lax.broadcasted_iota(I32, (t, t), 0)
        c = lax.broadcasted_iota(I32, (t, t), 1)
        scores, values = [], []
        for h in range(n_heads):
            sl = slice(h * DH, (h + 1) * DH)
            q, cq = head_q(h)
            s = lax.dot_general(q, kn_ref[0, :, sl], (((1,), (1,)), ((), ())), preferred_element_type=F32)
            scores.append(jnp.where(c <= r, s + (cq - cr_ref[0, h, :, past:past + t]), -jnp.inf))
            values.append(vn_ref[0, :, sl])
        _, l2, acc2 = softmax_step(scores, values, m, l, acc)
        out = acc2 / l2
        for h in range(n_heads):
            o_ref[0, :, h * DH:(h + 1) * DH] = out[h * t:(h + 1) * t].astype(BF16)


def _fox_sample(q, cache_k, cache_v, layer, k_new, v_new, c_pad, c_rows):
    b, t, hd = q.shape
    n_heads = hd // DH
    past = cache_k.shape[2]
    tk = min(CACHE_TILE, past)
    assert past % tk == 0 and past % t == 0 and tk % LANES == 0
    assert cache_k.shape[1:] == (b, past, n_heads, DH)
    kern = functools.partial(_fox_sample_kernel, n_heads=n_heads, t=t, tk=tk, past=past)
    per_b = lambda bi, kj: (bi, 0, 0)
    n_kb = past // tk
    cache_rows = lambda bi, kj: ((layer * b + bi) * n_kb + kj, 0)
    return pl.pallas_call(
        kern,
        grid=(b, n_kb),
        in_specs=[
            pl.BlockSpec((1, t, hd), per_b),
            pl.BlockSpec((tk * n_heads, DH), cache_rows),
            pl.BlockSpec((tk * n_heads, DH), cache_rows),
            pl.BlockSpec((1, t, hd), per_b),
            pl.BlockSpec((1, t, hd), per_b),
            pl.BlockSpec((1, t, LANES), lambda bi, kj: (bi, past // t, 0)),
            pl.BlockSpec((1, n_heads, 1, past + t), lambda bi, kj: (bi, 0, 0, 0)),
        ],
        out_specs=pl.BlockSpec((1, t, hd), per_b),
        out_shape=jax.ShapeDtypeStruct((b, t, hd), BF16),
        scratch_shapes=[
            pltpu.VMEM((n_heads * t, 1), F32),
            pltpu.VMEM((n_heads * t, 1), F32),
            pltpu.VMEM((n_heads * t, DH), F32),
        ],
        compiler_params=_cparams("arbitrary", "arbitrary"),
        name="fox_sample",
    )(q, cache_k.reshape(-1, DH), cache_v.reshape(-1, DH), k_new, v_new, c_pad, c_rows)


def _out_proj_kernel(a_ref, b_ref, wa_ref, wb_ref, r_ref, o_ref):
    o_ref[...] = (r_ref[...]
                  + jnp.dot(a_ref[...], wa_ref[...], preferred_element_type=F32)
                  + jnp.dot(b_ref[...], wb_ref[...], preferred_element_type=F32))


def _out_proj(a, b, w, res):
    t, ka = a.shape
    kb = b.shape[1]
    d = w.shape[1]
    tm = _row_tile(t, t)
    res = _view(res)
    assert ka == kb and res.n == t and res.row0 % tm == 0
    return pl.pallas_call(
        _out_proj_kernel,
        grid=(t // tm,),
        in_specs=[
            pl.BlockSpec((tm, ka), lambda i: (i, 0)),
            pl.BlockSpec((tm, kb), lambda i: (i, 0)),
            pl.BlockSpec((ka, d), lambda i: (0, 0)),
            pl.BlockSpec((kb, d), lambda i: (1, 0)),
            pl.BlockSpec((tm, d), lambda i: (i + res.row0 // tm, 0)),
        ],
        out_specs=pl.BlockSpec((tm, d), lambda i: (i, 0)),
        out_shape=jax.ShapeDtypeStruct((t, d), F32),
        compiler_params=_cparams("arbitrary"),
        name="out_proj",
    )(a, b, w, w, res.arr)


def _odd_in_kernel(x_ref, g_ref, w_ref, wc_ref, lg_ref, lb_ref, vg_ref, ws_ref, bs_ref, prev_ref,
                   yc_ref, yd_ref, vn_ref, st_ref,
                   xn_s, val_s, u_s, ext_s, conv_s, *, nb, lt, tiles_per_seq, taps, n_groups, lc):
    i = pl.program_id(0)
    j = pl.program_id(1)
    hist = taps - 1
    halo = ((hist + 7) // 8) * 8

    @pl.when(j == 0)
    def _():
        xn_s[...] = _rms(x_ref[...], g_ref[...]).astype(BF16)

    acc = jnp.dot(xn_s[...], w_ref[...], preferred_element_type=F32)

    @pl.when(j == 0)
    def _():
        val_s[...] = acc

    @pl.when(j == 1)
    def _():
        glu = val_s[...] * jax.nn.sigmoid(acc)
        for n in range(nb):
            ext_s[n, halo:halo + lt, :] = glu[n * lt:(n + 1) * lt, :]

        @pl.when(i % tiles_per_seq == 0)
        def _():
            for n in range(nb):
                ext_s[n, halo - hist:halo, :] = prev_ref[n]
                if halo > hist:
                    ext_s[n, 0:halo - hist, :] = jnp.zeros((halo - hist, ext_s.shape[2]), F32)

        dc = conv_s.shape[1]
        off0 = halo - hist
        win = ((off0 + hist + CONV_ROWS + 7) // 8) * 8
        for n in range(nb):
            for r0 in range(0, lt, CONV_ROWS):
                for c0 in range(0, dc, CONV_COLS):
                    cols = slice(c0, c0 + CONV_COLS)
                    blk = ext_s[n, r0:r0 + win, cols]
                    part = None
                    for b in range(8):
                        rolled = blk if b == 0 else pltpu.roll(blk, win - b, axis=0)
                        for a in range(win // 8):
                            s = 8 * a + b - off0
                            if 0 <= s < taps and 8 * a + CONV_ROWS <= win:
                                term = wc_ref[s:s + 1, cols] * rolled[8 * a:8 * a + CONV_ROWS, :]
                                part = term if part is None else part + term
                    conv_s[n * lt + r0:n * lt + r0 + CONV_ROWS, cols] = part
        for n in range(nb):
            conv = conv_s[n * lt:(n + 1) * lt, :]
            mu = jnp.mean(conv, axis=-1, keepdims=True)
            xc = conv - mu
            var = jnp.mean(xc * xc, axis=-1, keepdims=True)
            y = xc * lax.rsqrt(var + EPS) * lg_ref[...] + lb_ref[...]
            yc_ref[n * lt:(n + 1) * lt, :] = (y * jax.nn.sigmoid(y)).astype(BF16)
            tail = ext_s[n, halo - hist + lt:halo + lt, :]
            st_ref[n] = tail
            ext_s[n, halo - hist:halo, :] = tail

    @pl.when(j == 2)
    def _():
        u_s[...] = acc

    @pl.when(j == 3)
    def _():
        vn = _rms(acc, vg_ref[...])
        vn_ref[...] = vn
        vb = vn.astype(BF16)
        gw = vn.shape[1] // n_groups
        for c in range(vn.shape[0] // lc):
            rows = slice(c * lc, (c + 1) * lc)
            for g in range(n_groups):
                cols = slice(g * gw, (g + 1) * gw)
                gate = jnp.dot(ws_ref[g], vb[rows, cols], preferred_element_type=F32) + bs_ref[g]
                yd_ref[rows, cols] = (u_s[rows, cols] * gate).astype(BF16)


def _odd_in(x, g, w_in, w_conv, ln_g, ln_b, v_gain, w_s, b_s, prev, seq_len):
    x = _view(x)
    t, d = x.n, x.arr.shape[1]
    taps, dc = w_conv.shape
    assert w_in.shape[1] == 4 * dc
    n_groups = w_s.shape[0]
    tm = _row_tile(t, seq_len, cap=ROW_TILE if seq_len >= ROW_TILE else ROW_TILE // 2)
    lt = min(seq_len, tm)
    nb = tm // lt
    tiles_per_seq = seq_len // lt
    n_seq = t // seq_len
    lc = min(CHUNK_MLP, seq_len)
    assert lt % lc == 0 and lt % CONV_ROWS == 0 and dc % CONV_COLS == 0
    hist = taps - 1
    halo = ((hist + 7) // 8) * 8
    kern = functools.partial(_odd_in_kernel, nb=nb, lt=lt, tiles_per_seq=tiles_per_seq, taps=taps,
                             n_groups=n_groups, lc=lc)
    row = lambda i, j: (i, 0)
    const = lambda i, j: (0, 0)
    const3 = lambda i, j: (0, 0, 0)
    seq3 = lambda i, j: (i // tiles_per_seq, 0, 0)
    return pl.pallas_call(
        kern,
        grid=(t // tm, 4),
        in_specs=[
            pl.BlockSpec((tm, d), lambda i, j: (i + x.row0 // tm, 0)),
            pl.BlockSpec((1, d), const),
            pl.BlockSpec((d, dc), lambda i, j: (0, j)),
            pl.BlockSpec((taps, dc), const),
            pl.BlockSpec((1, dc), const),
            pl.BlockSpec((1, dc), const),
            pl.BlockSpec((1, dc), const),
            pl.BlockSpec((n_groups, lc, lc), const3),
            pl.BlockSpec((n_groups, lc, 1), const3),
            pl.BlockSpec((nb, hist, dc), seq3),
        ],
        out_specs=[
            pl.BlockSpec((tm, dc), row),
            pl.BlockSpec((tm, dc), row),
            pl.BlockSpec((tm, dc), row),
            pl.BlockSpec((nb, hist, dc), seq3),
        ],
        out_shape=[
            jax.ShapeDtypeStruct((t, dc), BF16),
            jax.ShapeDtypeStruct((t, dc), BF16),
            jax.ShapeDtypeStruct((t, dc), F32),
            jax.ShapeDtypeStruct((n_seq, hist, dc), F32),
        ],
        scratch_shapes=[
            pltpu.VMEM((tm, d), BF16),
            pltpu.VMEM((tm, dc), F32),
            pltpu.VMEM((tm, dc), F32),
            pltpu.VMEM((nb, halo + lt, dc), F32),
            pltpu.VMEM((tm, dc), F32),
        ],
        compiler_params=_cparams("arbitrary", "arbitrary"),
        name="odd_in",
    )(x.arr, g, w_in, w_conv, ln_g, ln_b, v_gain, w_s, b_s, prev)


def _mem_kv_kernel(m_ref, g_ref, wk_ref, wv_ref, kg_ref, k_ref, v_ref, *, n_heads):
    mn = _rms(m_ref[...], g_ref[...]).astype(BF16)
    k = jnp.dot(mn, wk_ref[...], preferred_element_type=F32)
    for h in range(n_heads):
        seg = k[:, h * DH:(h + 1) * DH]
        k_ref[:, h * DH:(h + 1) * DH] = seg * lax.rsqrt(jnp.mean(seg * seg, axis=-1, keepdims=True) + EPS) * kg_ref[...]
    v_ref[...] = jnp.dot(mn, wv_ref[...], preferred_element_type=F32)


def _mem_kv(mem, g, wk, wv, k_gain):
    t, d = mem.shape
    dx = wk.shape[1]
    tm = _row_tile(t, t)
    return pl.pallas_call(
        functools.partial(_mem_kv_kernel, n_heads=dx // DH),
        grid=(t // tm,),
        in_specs=[
            pl.BlockSpec((tm, d), lambda i: (i, 0)),
            pl.BlockSpec((1, d), lambda i: (0, 0)),
            pl.BlockSpec((d, dx), lambda i: (0, 0)),
            pl.BlockSpec((d, dx), lambda i: (0, 0)),
            pl.BlockSpec((1, DH), lambda i: (0, 0)),
        ],
        out_specs=[pl.BlockSpec((tm, dx), lambda i: (i, 0))] * 2,
        out_shape=[jax.ShapeDtypeStruct((t, dx), F32)] * 2,
        compiler_params=_cparams("arbitrary"),
        name="mem_kv",
    )(mem, g, wk, wv, k_gain)


def _mem_attn_kernel(x_ref, g_ref, wq_ref, qg_ref, k_ref, v_ref, wo_ref, o_ref, att_s, *, nb, lt, n_heads):
    x = x_ref[...]
    xn = _rms(x, g_ref[...]).astype(BF16)
    q = jnp.dot(xn, wq_ref[...], preferred_element_type=F32)
    for h in range(n_heads):
        sl = slice(h * DH, (h + 1) * DH)
        seg = q[:, sl]
        qn = seg * lax.rsqrt(jnp.mean(seg * seg, axis=-1, keepdims=True) + EPS) * qg_ref[...]
        qb = (qn * (DH ** -0.5)).astype(BF16)
        for n in range(nb):
            rows = slice(n * lt, (n + 1) * lt)
            k = k_ref[n, :, sl].astype(BF16)
            v = v_ref[n, :, sl].astype(BF16)
            s = lax.dot_general(qb[rows], k, (((1,), (1,)), ((), ())), preferred_element_type=F32)
            p = jnp.exp(s - jnp.max(s, axis=1, keepdims=True))
            o = jnp.dot(p.astype(BF16), v, preferred_element_type=F32) / jnp.sum(p, axis=1, keepdims=True)
            att_s[rows, sl] = o.astype(BF16)
    o_ref[...] = x + jnp.dot(att_s[...], wo_ref[...], preferred_element_type=F32)


def _mem_attn(x, g, wq, q_gain, k, v, wo, seq_len):
    t, d = x.shape
    dx = wq.shape[1]
    n_mem = k.shape[1]
    tm = _row_tile(t, seq_len, cap=256 if seq_len < 256 else ROW_TILE)
    lt = min(seq_len, tm)
    nb = tm // lt
    tiles_per_seq = seq_len // lt
    kern = functools.partial(_mem_attn_kernel, nb=nb, lt=lt, n_heads=dx // DH)
    return pl.pallas_call(
        kern,
        grid=(t // tm,),
        in_specs=[
            pl.BlockSpec((tm, d), lambda i: (i, 0)),
            pl.BlockSpec((1, d), lambda i: (0, 0)),
            pl.BlockSpec((d, dx), lambda i: (0, 0)),
            pl.BlockSpec((1, DH), lambda i: (0, 0)),
            pl.BlockSpec((nb, n_mem, dx), lambda i: (i // tiles_per_seq, 0, 0)),
            pl.BlockSpec((nb, n_mem, dx), lambda i: (i // tiles_per_seq, 0, 0)),
            pl.BlockSpec((dx, d), lambda i: (0, 0)),
        ],
        out_specs=pl.BlockSpec((tm, d), lambda i: (i, 0)),
        out_shape=jax.ShapeDtypeStruct((t, d), F32),
        scratch_shapes=[pltpu.VMEM((tm, dx), BF16)],
        compiler_params=_cparams("arbitrary"),
        name="mem_attn",
    )(x, g, wq, q_gain, k, v, wo)


GROUP_LANE0 = N_EXPERTS
PAIRS_PER_GROUP = EXP_PER_GROUP * (EXP_PER_GROUP - 1) // 2
N_CLASSES = N_GROUPS * PAIRS_PER_GROUP
META_C, META_R, META_G = 0, 1, 2
MOE_BLOCK = 128


def _router_kernel(x_ref, g_ref, w_ref, b_ref, meta_ref, cnt_ref, *, tm):
    @pl.when(pl.program_id(0) == 0)
    def _():
        cnt_ref[...] = jnp.zeros_like(cnt_ref)

    xn = _rms(x_ref[...], g_ref[...]).astype(BF16)
    logits = jnp.dot(xn, w_ref[...], preferred_element_type=F32) + b_ref[...]
    lane = lax.broadcasted_iota(I32, (tm, LANES), 1)
    big = jnp.int32(LANES)

    def first_argmax(vals):
        top = jnp.max(vals, axis=1, keepdims=True)
        return top, jnp.min(jnp.where(vals == top, lane, big), axis=1, keepdims=True)

    is_grp = (lane >= GROUP_LANE0) & (lane < GROUP_LANE0 + N_GROUPS)
    lg = jnp.where(is_grp, logits, -jnp.inf)
    g_top, g_lane = first_argmax(lg)
    p_sel = 1.0 / jnp.sum(jnp.where(is_grp, jnp.exp(lg - g_top), 0.0), axis=1, keepdims=True)
    e0 = (g_lane - GROUP_LANE0) * EXP_PER_GROUP
    le = jnp.where((lane >= e0) & (lane < e0 + EXP_PER_GROUP), logits, -jnp.inf)
    v1, i1 = first_argmax(le)
    v2, i2 = first_argmax(jnp.where(lane == i1, -jnp.inf, le))
    e = jnp.exp(v2 - v1)
    g1 = p_sel / (1.0 + e)
    g2 = p_sel * e / (1.0 + e)

    swap = i2 < i1
    lo = jnp.minimum(i1, i2) - e0
    hi = jnp.maximum(i1, i2) - e0
    pair = jnp.right_shift(lo * (2 * EXP_PER_GROUP - 1 - lo), 1) + (hi - lo - 1)
    cls = (g_lane - GROUP_LANE0) * PAIRS_PER_GROUP + pair
    g_lo = jnp.where(swap, g2, g1)
    g_hi = jnp.where(swap, g1, g2)

    sel = lane == cls
    onehot = jnp.where(sel, 1.0, 0.0)
    r = lax.broadcasted_iota(I32, (tm, tm), 0)
    c = lax.broadcasted_iota(I32, (tm, tm), 1)
    earlier = jnp.where(c < r, 1.0, 0.0).astype(BF16)
    before = jnp.dot(earlier, onehot.astype(BF16), preferred_element_type=F32) + cnt_ref[...]
    rank = jnp.sum(jnp.where(sel, before, 0.0), axis=1, keepdims=True)
    cnt_ref[...] = cnt_ref[...] + jnp.sum(onehot, axis=0, keepdims=True)

    meta = jnp.zeros((tm, LANES), F32)
    for k, val in ((META_C, cls.astype(F32)), (META_R, rank), (META_G, g_lo), (META_G + 1, g_hi)):
        meta = jnp.where(lane == k, val, meta)
    meta_ref[...] = meta


def _router(x, g, w_rt, b_rt):
    t, d = x.shape
    tm = ROW_TILE if t % ROW_TILE == 0 else GATHER_TILE
    assert t % tm == 0
    return pl.pallas_call(
        functools.partial(_router_kernel, tm=tm),
        grid=(t // tm,),
        in_specs=[
            pl.BlockSpec((tm, d), lambda i: (i, 0)),
            pl.BlockSpec((1, d), lambda i: (0, 0)),
            pl.BlockSpec((d, LANES), lambda i: (0, 0)),
            pl.BlockSpec((1, LANES), lambda i: (0, 0)),
        ],
        out_specs=[pl.BlockSpec((tm, LANES), lambda i: (i, 0)), pl.BlockSpec((1, LANES), lambda i: (0, 0))],
        out_shape=[jax.ShapeDtypeStruct((t, LANES), F32), jax.ShapeDtypeStruct((1, LANES), F32)],
        compiler_params=_cparams("arbitrary"),
        name="moe_router",
    )(x, g, w_rt, b_rt)


def _row_copy(src, src_row, dst, dst_row, sem):
    return pltpu.make_async_copy(src.at[pl.ds(src_row, 1)], dst.at[pl.ds(dst_row, 1)], sem)


def _dispatch_kernel(pos_ref, x_ref, g_ref, meta_ref, zero_ref, xb_ref, xn_s, sem, *, tm, d):
    del zero_ref
    i = pl.program_id(0)
    n_steps = pl.num_programs(0)
    slot = i % 2
    base = i * tm

    def drain(sl):
        for _ in range(tm):
            _row_copy(xn_s.at[sl], 0, xb_ref, 0, sem.at[sl]).wait()

    @pl.when(i >= 2)
    def _():
        drain(slot)

    xn_s[slot, :, :d] = _rms(x_ref[...], g_ref[...])
    xn_s[slot, :, d:] = meta_ref[...]

    def issue(r, _):
        _row_copy(xn_s.at[slot], r, xb_ref, pos_ref[base + r], sem.at[slot]).start()
        return 0

    lax.fori_loop(0, tm, issue, 0, unroll=COPY_UNROLL)

    @pl.when(i == n_steps - 1)
    def _():
        @pl.when(n_steps >= 2)
        def _():
            drain(1 - slot)

        drain(slot)


def _dispatch(pos, x, g, meta, n_rows):
    t, d = x.shape
    tm = min(GATHER_TILE, t)
    assert t % tm == 0
    zeros = jnp.zeros((n_rows, d + LANES), F32)
    return pl.pallas_call(
        functools.partial(_dispatch_kernel, tm=tm, d=d),
        grid_spec=pltpu.PrefetchScalarGridSpec(
            num_scalar_prefetch=1,
            grid=(t // tm,),
            in_specs=[
                pl.BlockSpec((tm, d), lambda i, pos: (i, 0)),
                pl.BlockSpec((1, d), lambda i, pos: (0, 0)),
                pl.BlockSpec((tm, LANES), lambda i, pos: (i, 0)),
                pl.BlockSpec(memory_space=pl.ANY),
            ],
            out_specs=pl.BlockSpec(memory_space=pl.ANY),
            scratch_shapes=[pltpu.VMEM((2, tm, d + LANES), F32), pltpu.SemaphoreType.DMA((2,))],
        ),
        out_shape=jax.ShapeDtypeStruct((n_rows, d + LANES), F32),
        input_output_aliases={4: 0},
        compiler_params=_cparams("arbitrary"),
        name="moe_dispatch",
    )(pos, x, g, meta, zeros)


def _expert_kernel(ba_ref, bb_ref, nu_ref, x_ref, wga_ref, wua_ref, wda_ref, wgb_ref, wub_ref, wdb_ref,
                   y_ref, *, d):
    del ba_ref, bb_ref
    used = pl.program_id(0) < nu_ref[0]

    @pl.when(used)
    def _():
        x = x_ref[:, :d].astype(BF16)
        rec = x_ref[:, d:]

        def ffn(wg_ref, wu_ref, wd_ref):
            gate = jnp.dot(x, wg_ref[0, 0], preferred_element_type=F32)
            up = jnp.dot(x, wu_ref[0, 0], preferred_element_type=F32)
            hid = (gate * jax.nn.sigmoid(gate) * up).astype(BF16)
            return jnp.dot(hid, wd_ref[0, 0], preferred_element_type=F32)

        y_ref[...] = (rec[:, META_G:META_G + 1] * ffn(wga_ref, wua_ref, wda_ref)
                      + rec[:, META_G + 1:META_G + 2] * ffn(wgb_ref, wub_ref, wdb_ref))

    @pl.when(jnp.logical_not(used))
    def _():
        y_ref[...] = jnp.zeros_like(y_ref)


def _experts(blk_a, blk_b, n_used, xb, w_gate, w_up, w_down, layer, bm):
    de, d = w_down.shape[2:]
    p = xb.shape[0]
    n_blocks = p // bm

    def xmap(b, ba, bb, nu):
        return (jnp.minimum(b, jnp.maximum(nu[0] - 1, 0)), 0)

    def amap(b, ba, bb, nu):
        return (layer, ba[b], 0, 0)

    def bmap(b, ba, bb, nu):
        return (layer, bb[b], 0, 0)

    return pl.pallas_call(
        functools.partial(_expert_kernel, d=d),
        grid_spec=pltpu.PrefetchScalarGridSpec(
            num_scalar_prefetch=3,
            grid=(n_blocks,),
            in_specs=[
                pl.BlockSpec((bm, d + LANES), xmap),
                pl.BlockSpec((1, 1, d, de), amap),
                pl.BlockSpec((1, 1, d, de), amap),
                pl.BlockSpec((1, 1, de, d), amap),
                pl.BlockSpec((1, 1, d, de), bmap),
                pl.BlockSpec((1, 1, d, de), bmap),
                pl.BlockSpec((1, 1, de, d), bmap),
            ],
            out_specs=pl.BlockSpec((bm, d), lambda b, ba, bb, nu: (b, 0)),
        ),
        out_shape=jax.ShapeDtypeStruct((p, d), F32),
        compiler_params=_cparams("arbitrary"),
        name="moe_experts",
    )(blk_a, blk_b, n_used, xb, w_gate, w_up, w_down, w_gate, w_up, w_down)


def _combine_kernel(pos_ref, x_ref, yb_ref, o_ref, buf_s, sem, *, tm):
    i = pl.program_id(0)
    slot = i % 2

    def issue(tile, sl):
        base = tile * tm

        def start_row(r, _):
            _row_copy(yb_ref, pos_ref[base + r], buf_s.at[sl], r, sem.at[sl]).start()
            return 0

        lax.fori_loop(0, tm, start_row, 0, unroll=COPY_UNROLL)

    @pl.when(i == 0)
    def _():
        issue(0, 0)

    @pl.when(i + 1 < pl.num_programs(0))
    def _():
        issue(i + 1, 1 - slot)

    for _ in range(tm):
        _row_copy(yb_ref, 0, buf_s.at[slot], 0, sem.at[slot]).wait()
    o_ref[...] = x_ref[...] + buf_s[slot]


def _combine(pos, x, yb):
    t, d = x.shape
    tm = min(GATHER_TILE, t)
    return pl.pallas_call(
        functools.partial(_combine_kernel, tm=tm),
        grid_spec=pltpu.PrefetchScalarGridSpec(
            num_scalar_prefetch=1,
            grid=(t // tm,),
            in_specs=[
                pl.BlockSpec((tm, d), lambda i, pos: (i, 0)),
                pl.BlockSpec(memory_space=pl.ANY),
            ],
            out_specs=pl.BlockSpec((tm, d), lambda i, pos: (i, 0)),
            scratch_shapes=[pltpu.VMEM((2, tm, d), F32), pltpu.SemaphoreType.DMA((2,))],
        ),
        out_shape=jax.ShapeDtypeStruct((t, d), F32),
        compiler_params=_cparams("arbitrary"),
        name="moe_combine",
    )(pos, x, yb)


def _moe(x, g, w_rt, b_rt, w_gate, w_up, w_down, layer):
    t, d = x.shape
    meta, cnt = _router(x, g, w_rt, b_rt)
    bm = MOE_BLOCK
    counts = cnt[0, :N_CLASSES].astype(I32)
    padded = ((counts + bm - 1) // bm) * bm
    pend = jnp.cumsum(padded)
    pstart = pend - padded
    n_blocks = (t + N_CLASSES * (bm - 1) + bm - 1) // bm
    cls = meta[:, META_C].astype(I32)
    rank = meta[:, META_R].astype(I32)
    onehot = cls[:, None] == jnp.arange(N_CLASSES, dtype=I32)[None, :]
    pos = jnp.sum(jnp.where(onehot, pstart[None, :], 0), axis=-1) + rank
    blk_row0 = jnp.arange(n_blocks, dtype=I32) * bm
    blk_cls = jnp.minimum(jnp.sum((pend[None, :] <= blk_row0[:, None]).astype(I32), axis=1), N_CLASSES - 1)
    pairs = [(grp * EXP_PER_GROUP + a, grp * EXP_PER_GROUP + b) for grp in range(N_GROUPS)
             for a in range(EXP_PER_GROUP) for b in range(a + 1, EXP_PER_GROUP)]
    blk_a = jnp.asarray([p[0] for p in pairs], I32)[blk_cls]
    blk_b = jnp.asarray([p[1] for p in pairs], I32)[blk_cls]
    n_used = (pend[-1:] // bm).astype(I32)
    xb = _dispatch(pos, x, g, meta, n_blocks * bm)
    yb = _experts(blk_a, blk_b, n_used, xb, w_gate, w_up, w_down, layer, bm)
    return _combine(pos, x, yb)


def _pad_lanes(a):
    return jnp.pad(a, [(0, 0)] * (a.ndim - 1) + [(0, LANES - a.shape[-1])])


def kernel(x_prompt, x_sample, mem_prompt, cache_fox_k, cache_fox_v, cache_fox_logf, state_conv_a, state_conv_c, cache_mem_k, cache_mem_v, norm_mix, norm_xmem, norm_mem, norm_ffn, even_w_in, even_w_conv_a, fox_q_gain, fox_k_gain, fox_f_bias, even_w_out, odd_w_in, odd_w_conv_c, conf_ln_gain, conf_ln_bias, sgu_v_gain, sgu_w, sgu_b, odd_w_out, xmem_wq, xmem_wk, xmem_wv, xmem_q_gain, xmem_k_gain, xmem_wo, moe_w_group, moe_b_group, moe_w_router, moe_b_router, moe_w_gate, moe_w_up, moe_w_down):
    bp, sp, d = x_prompt.shape
    bs, ss, _ = x_sample.shape
    depth = norm_mix.shape[0]
    n_mem = mem_prompt.shape[1]
    d_a = even_w_conv_a.shape[2]
    conv_a = even_w_conv_a.shape[1]
    conv_c = odd_w_conv_c.shape[1]
    d_c = odd_w_conv_c.shape[2]
    h_b = fox_f_bias.shape[1]
    d_b = h_b * DH
    d_x = xmem_wq.shape[2]
    h_mem = d_x // DH
    n_main = 3 * d_a + 3 * d_b

    xp = x_prompt.reshape(bp * sp, d)
    xs = x_sample.reshape(bs * ss, d)
    mem2 = mem_prompt.reshape(bp * n_mem, d)
    moe_w = (moe_w_gate.astype(BF16), moe_w_up.astype(BF16), moe_w_down.astype(BF16))
    outs = {k: [] for k in ('fk_p', 'fv_p', 'fl_p', 'fk_s', 'fv_s', 'fl_s', 'ca_p', 'ca_s', 'cc_p', 'cc_s', 'cv_s', 'mk_p', 'mv_p')}

    for i in range(depth):
        g_mix = norm_mix[i][None]
        if i % 2 == 0:
            e = i // 2
            w_in = even_w_in[e]
            w_main = w_in[:, :n_main].astype(BF16)
            w_f = _pad_lanes(w_in[:, n_main:]).astype(BF16)
            f_bias = _pad_lanes(fox_f_bias[e][None])
            w_out = even_w_out[e].astype(BF16)
            args = (g_mix, w_main, w_f, f_bias, even_w_conv_a[e], fox_q_gain[e][None], fox_k_gain[e][None])

            ya, q, k, kb, v, vb, lf, st = _even_in(xp, *args, jnp.zeros((bp, conv_a - 1, d_a), F32), sp)
            lf3 = lf.reshape(bp, sp, LANES)
            c = _cumsum_time(lf3)
            c_rows = jnp.swapaxes(c[:, :, :h_b], 1, 2)[:, :, None, :]
            yb = _fox_prompt(q.reshape(bp, sp, d_b), kb.reshape(bp, sp, d_b), vb.reshape(bp, sp, d_b), c, c_rows)
            xp = _out_proj(ya, yb.reshape(bp * sp, d_b), w_out, xp)
            outs['fk_p'].append(k.reshape(bp, sp, h_b, DH))
            outs['fv_p'].append(v.reshape(bp, sp, h_b, DH))
            outs['fl_p'].append(lf3[:, :, :h_b])
            outs['ca_p'].append(st)

            ya, q, k, kb, v, vb, lf, st = _even_in(xs, *args, state_conv_a[e], ss)
            lf3 = lf.reshape(bs, ss, LANES)
            lf_all = jnp.concatenate([_pad_lanes(cache_fox_logf[e]), lf3], axis=1)
            c = _cumsum_time(lf_all)
            c_rows = jnp.swapaxes(c[:, :, :h_b], 1, 2)[:, :, None, :]
            yb = _fox_sample(q.reshape(bs, ss, d_b), cache_fox_k, cache_fox_v, e, kb.reshape(bs, ss, d_b),
                             vb.reshape(bs, ss, d_b), c, c_rows)
            xs = _out_proj(ya, yb.reshape(bs * ss, d_b), w_out, xs)
            outs['fk_s'].append(k.reshape(bs, ss, h_b, DH))
            outs['fv_s'].append(v.reshape(bs, ss, h_b, DH))
            outs['fl_s'].append(lf3[:, :, :h_b])
            outs['ca_s'].append(st)
        else:
            o = i // 2
            w_in = odd_w_in[o].astype(BF16)
            w_out = odd_w_out[o].astype(BF16)
            tril = jnp.tril(jnp.ones(sgu_w.shape[2:], dtype=bool))
            w_s = jnp.where(tril[None], sgu_w[o], 0).astype(BF16)
            b_s = sgu_b[o][:, :, None]

            def odd(x, prev, seq_len):
                lc = min(CHUNK_MLP, seq_len)
                return _odd_in(x, g_mix, w_in, odd_w_conv_c[o], conf_ln_gain[o][None], conf_ln_bias[o][None],
                               sgu_v_gain[o][None], w_s[:, :lc, :lc], b_s[:, :lc], prev, seq_len)

            yc, yd, _, st = odd(xp, jnp.zeros((bp, conv_c - 1, d_c), F32), sp)
            xp = _out_proj(yc, yd, w_out, xp)
            outs['cc_p'].append(st)
            yc, yd, vn, st = odd(xs, state_conv_c[o], ss)
            xs = _out_proj(yc, yd, w_out, xs)
            outs['cc_s'].append(st)
            outs['cv_s'].append(vn.reshape(bs, ss, -1))

        g_x = norm_xmem[i][None]
        wq = xmem_wq[i].astype(BF16)
        wo = xmem_wo[i].astype(BF16)
        q_gain = xmem_q_gain[i][None]
        mk, mv = _mem_kv(mem2, norm_mem[i][None], xmem_wk[i].astype(BF16), xmem_wv[i].astype(BF16), xmem_k_gain[i][None])
        mk3 = mk.reshape(bp, n_mem, d_x)
        mv3 = mv.reshape(bp, n_mem, d_x)
        outs['mk_p'].append(mk3.reshape(bp, n_mem, h_mem, DH))
        outs['mv_p'].append(mv3.reshape(bp, n_mem, h_mem, DH))
        xp = _mem_attn(xp, g_x, wq, q_gain, mk3, mv3, wo, sp)
        xs = _mem_attn(xs, g_x, wq, q_gain, cache_mem_k[i].reshape(bs, n_mem, d_x),
                       cache_mem_v[i].reshape(bs, n_mem, d_x), wo, ss)

        g_f = norm_ffn[i][None]
        w_rt = _pad_lanes(jnp.concatenate([moe_w_router[i], moe_w_group[i]], axis=1)).astype(BF16)
        b_rt = _pad_lanes(jnp.concatenate([moe_b_router[i], moe_b_group[i]])[None])
        x_all = _moe(jnp.concatenate([xp, xs], axis=0), g_f, w_rt, b_rt, *moe_w, i)
        xp = _Rows(x_all, 0, bp * sp)
        xs = _Rows(x_all, bp * sp, bs * ss)

    st = lambda key: jnp.stack(outs[key])
    return (x_all[:bp * sp].reshape(bp, sp, d), x_all[bp * sp:].reshape(bs, ss, d),
            st('fk_p'), st('fv_p'), st('fl_p'),
            st('fk_s'), st('fv_s'), st('fl_s'),
            st('ca_p'), st('ca_s'),
            st('cc_p'), st('cc_s'),
            st('cv_s'),
            st('mk_p'), st('mv_p'))
```

```python
import functools
from typing import NamedTuple

import jax
import jax.numpy as jnp
from jax import lax
from jax.experimental import pallas as pl
from jax.experimental.pallas import tpu as pltpu

F32 = jnp.float32
BF16 = jnp.bfloat16
I32 = jnp.int32

EPS = 1e-6
DH = 128
LANES = 128
LOG2E = 1.4426950408889634
CHUNK_MLP = 128
N_GROUPS = 4
EXP_PER_GROUP = 8
N_EXPERTS = N_GROUPS * EXP_PER_GROUP
V7X_VMEM_LIMIT = 56 * 1024 * 1024
ROW_TILE = 512
ATT_TILE = 512
CACHE_TILE = 1024
CONV_ROWS, CONV_COLS = 32, 256
COPY_UNROLL = 8
GATHER_TILE = 256


def _cparams(*sem):
    return pltpu.CompilerParams(dimension_semantics=sem, vmem_limit_bytes=V7X_VMEM_LIMIT)


def _rms(x, g):
    return x * lax.rsqrt(jnp.mean(x * x, axis=-1, keepdims=True) + EPS) * g


class _Rows(NamedTuple):
    arr: jax.Array
    row0: int
    n: int


def _view(x):
    return x if isinstance(x, _Rows) else _Rows(x, 0, x.shape[0])


def _row_tile(n_rows, seq_len, cap=ROW_TILE):
    if seq_len >= cap:
        assert seq_len % cap == 0
        return cap
    nb = max(1, cap // seq_len)
    n_seq = n_rows // seq_len
    while n_seq % nb:
        nb -= 1
    return nb * seq_len


def _even_in_kernel(x_ref, g_ref, w_ref, wf_ref, fb_ref, wc_ref, qg_ref, kg_ref, prev_ref,
                    ya_ref, q_ref, k_ref, kb_ref, v_ref, vb_ref, lf_ref, st_ref,
                    xn_s, ab_s, ac_s, ext_s, *, nb, lt, tiles_per_seq, n_heads):
    i = pl.program_id(0)
    j = pl.program_id(1)
    halo = 8

    @pl.when(j == 0)
    def _():
        xn = _rms(x_ref[...], g_ref[...]).astype(BF16)
        xn_s[...] = xn
        z = jnp.dot(xn, wf_ref[...], preferred_element_type=F32) + fb_ref[...]
        lf_ref[...] = jnp.minimum(z, 0.0) - jnp.log1p(jnp.exp(-jnp.abs(z)))

    acc = jnp.dot(xn_s[...], w_ref[...], preferred_element_type=F32)

    @pl.when(j == 0)
    def _():
        ab_s[...] = acc

    @pl.when(j == 1)
    def _():
        ac_s[...] = acc

    @pl.when(j == 2)
    def _():
        gated = ac_s[...] * acc
        for n in range(nb):
            ext_s[n, halo:halo + lt, :] = gated[n * lt:(n + 1) * lt, :]

        @pl.when(i % tiles_per_seq == 0)
        def _():
            for n in range(nb):
                ext_s[n, halo - 2:halo, :] = prev_ref[n]

        w = wc_ref[...]
        for n in range(nb):
            conv = (w[0:1, :] * ext_s[n, halo - 2:halo - 2 + lt, :]
                    + w[1:2, :] * ext_s[n, halo - 1:halo - 1 + lt, :]
                    + w[2:3, :] * ext_s[n, halo:halo + lt, :])
            ya_ref[n * lt:(n + 1) * lt, :] = (ab_s[n * lt:(n + 1) * lt, :] * conv).astype(BF16)
            tail = ext_s[n, halo - 2 + lt:halo + lt, :]
            st_ref[n] = tail
            ext_s[n, halo - 2:halo, :] = tail

    def head_norm(gain):
        for h in range(n_heads):
            seg = acc[:, h * DH:(h + 1) * DH]
            yield h, seg * lax.rsqrt(jnp.mean(seg * seg, axis=-1, keepdims=True) + EPS) * gain

    @pl.when(j == 3)
    def _():
        for h, qn in head_norm(qg_ref[...]):
            q_ref[:, h * DH:(h + 1) * DH] = (qn * (DH ** -0.5 * LOG2E)).astype(BF16)

    @pl.when(j == 4)
    def _():
        for h, kn in head_norm(kg_ref[...]):
            k_ref[:, h * DH:(h + 1) * DH] = kn
            kb_ref[:, h * DH:(h + 1) * DH] = kn.astype(BF16)

    @pl.when(j == 5)
    def _():
        v_ref[...] = acc
        vb_ref[...] = acc.astype(BF16)


def _even_in(x, g, w_main, w_f, f_bias, w_conv, q_gain, k_gain, prev, seq_len):
    x = _view(x)
    t, d = x.n, x.arr.shape[1]
    dc = w_conv.shape[1]
    n_col = w_main.shape[1] // dc
    assert n_col == 6
    n_heads = dc // DH
    tm = _row_tile(t, seq_len)
    lt = min(seq_len, tm)
    nb = tm // lt
    tiles_per_seq = seq_len // lt
    n_seq = t // seq_len
    kern = functools.partial(_even_in_kernel, nb=nb, lt=lt, tiles_per_seq=tiles_per_seq, n_heads=n_heads)
    row = lambda i, j: (i, 0)
    const = lambda i, j: (0, 0)
    seq3 = lambda i, j: (i // tiles_per_seq, 0, 0)
    return pl.pallas_call(
        kern,
        grid=(t // tm, n_col),
        in_specs=[
            pl.BlockSpec((tm, d), lambda i, j: (i + x.row0 // tm, 0)),
            pl.BlockSpec((1, d), const),
            pl.BlockSpec((d, dc), lambda i, j: (0, j)),
            pl.BlockSpec((d, LANES), const),
            pl.BlockSpec((1, LANES), const),
            pl.BlockSpec(w_conv.shape, const),
            pl.BlockSpec((1, DH), const),
            pl.BlockSpec((1, DH), const),
            pl.BlockSpec((nb, 2, dc), seq3),
        ],
        out_specs=[
            pl.BlockSpec((tm, dc), row),
            pl.BlockSpec((tm, dc), row),
            pl.BlockSpec((tm, dc), row),
            pl.BlockSpec((tm, dc), row),
            pl.BlockSpec((tm, dc), row),
            pl.BlockSpec((tm, dc), row),
            pl.BlockSpec((tm, LANES), row),
            pl.BlockSpec((nb, 2, dc), seq3),
        ],
        out_shape=[
            jax.ShapeDtypeStruct((t, dc), BF16),
            jax.ShapeDtypeStruct((t, dc), BF16),
            jax.ShapeDtypeStruct((t, dc), F32),
            jax.ShapeDtypeStruct((t, dc), BF16),
            jax.ShapeDtypeStruct((t, dc), F32),
            jax.ShapeDtypeStruct((t, dc), BF16),
            jax.ShapeDtypeStruct((t, LANES), F32),
            jax.ShapeDtypeStruct((n_seq, 2, dc), F32),
        ],
        scratch_shapes=[
            pltpu.VMEM((tm, d), BF16),
            pltpu.VMEM((tm, dc), F32),
            pltpu.VMEM((tm, dc), F32),
            pltpu.VMEM((nb, 8 + lt, dc), F32),
        ],
        compiler_params=_cparams("arbitrary", "arbitrary"),
        name="even_in",
    )(x.arr, g, w_main, w_f, f_bias, w_conv, q_gain, k_gain, prev)


def _cumsum_kernel(lf_ref, c_ref, carry_s, *, ts):
    @pl.when(pl.program_id(1) == 0)
    def _():
        carry_s[...] = jnp.zeros_like(carry_s)

    x = lf_ref[0]
    row = lax.broadcasted_iota(I32, x.shape, 0)
    d = 1
    while d < ts:
        x = x + jnp.where(row >= d, pltpu.roll(x, d, axis=0), 0.0)
        d *= 2
    x = x + carry_s[...]
    c_ref[0] = x * LOG2E
    carry_s[...] = x[ts - 1:ts, :]


def _time_tile(s, cap=1024):
    ts = min(s, cap)
    while s % ts or ts % 8:
        ts -= 1
    return ts


def _cumsum_time(lf):
    b, s, _ = lf.shape
    ts = _time_tile(s)
    return pl.pallas_call(
        functools.partial(_cumsum_kernel, ts=ts),
        grid=(b, s // ts),
        in_specs=[pl.BlockSpec((1, ts, LANES), lambda bi, si: (bi, si, 0))],
        out_specs=pl.BlockSpec((1, ts, LANES), lambda bi, si: (bi, si, 0)),
        out_shape=jax.ShapeDtypeStruct(lf.shape, F32),
        scratch_shapes=[pltpu.VMEM((1, LANES), F32)],
        compiler_params=_cparams("arbitrary", "arbitrary"),
        name="cumsum_time",
    )(lf)


def _online_softmax_step(s, v, m, l, acc):
    m_new = jnp.maximum(m, jnp.max(s, axis=1, keepdims=True))
    p = jnp.exp2(s - m_new)
    alpha = jnp.exp2(m - m_new)
    l = alpha * l + jnp.sum(p, axis=1, keepdims=True)
    acc = alpha * acc + jnp.dot(p.astype(BF16), v, preferred_element_type=F32)
    return m_new, l, acc


def _fox_prompt_kernel(q_ref, k_ref, v_ref, cq_ref, ck_ref, o_ref, *, tq):
    h = pl.program_id(1)
    qi = pl.program_id(2)
    q = q_ref[0]
    lane = lax.broadcasted_iota(I32, (tq, LANES), 1)
    cq = jnp.sum(jnp.where(lane == h, cq_ref[0], 0.0), axis=1, keepdims=True)

    def scores(kj):
        ks = pl.multiple_of(kj * tq, tq)
        k = k_ref[0, pl.ds(ks, tq), :]
        v = v_ref[0, pl.ds(ks, tq), :]
        s = lax.dot_general(q, k, (((1,), (1,)), ((), ())), preferred_element_type=F32)
        return s + (cq - ck_ref[0, 0, :, pl.ds(ks, tq)]), v

    def past_block(kj, carry):
        s, v = scores(kj)
        return _online_softmax_step(s, v, *carry)

    init = (jnp.full((tq, 1), -jnp.inf, F32), jnp.zeros((tq, 1), F32), jnp.zeros((tq, DH), F32))
    m, l, acc = lax.fori_loop(0, qi, past_block, init)
    s, v = scores(qi)
    r = lax.broadcasted_iota(I32, (tq, tq), 0)
    c = lax.broadcasted_iota(I32, (tq, tq), 1)
    s = jnp.where(c <= r, s, -jnp.inf)
    m, l, acc = _online_softmax_step(s, v, m, l, acc)
    o_ref[0] = (acc / l).astype(BF16)


def _fox_prompt(q, k, v, c_pad, c_rows):
    b, s, hd = q.shape
    n_heads = hd // DH
    tq = min(ATT_TILE, s)
    assert s % tq == 0
    return pl.pallas_call(
        functools.partial(_fox_prompt_kernel, tq=tq),
        grid=(b, n_heads, s // tq),
        in_specs=[
            pl.BlockSpec((1, tq, DH), lambda bi, h, qi: (bi, qi, h)),
            pl.BlockSpec((1, s, DH), lambda bi, h, qi: (bi, 0, h)),
            pl.BlockSpec((1, s, DH), lambda bi, h, qi: (bi, 0, h)),
            pl.BlockSpec((1, tq, LANES), lambda bi, h, qi: (bi, qi, 0)),
            pl.BlockSpec((1, 1, 1, s), lambda bi, h, qi: (bi, h, 0, 0)),
        ],
        out_specs=pl.BlockSpec((1, tq, DH), lambda bi, h, qi: (bi, qi, h)),
        out_shape=jax.ShapeDtypeStruct((b, s, hd), BF16),
        compiler_params=_cparams("arbitrary", "arbitrary", "arbitrary"),
        name="fox_prompt",
    )(q, k, v, c_pad, c_rows)


def _fox_sample_kernel(q_ref, ck_ref, cv_ref, kn_ref, vn_ref, cq_ref, cr_ref, o_ref,
                       m_s, l_s, acc_s, *, n_heads, t, tk, past):
    kj = pl.program_id(1)
    last = pl.num_programs(1) - 1

    @pl.when(kj == 0)
    def _():
        m_s[...] = jnp.full_like(m_s, -jnp.inf)
        l_s[...] = jnp.zeros_like(l_s)
        acc_s[...] = jnp.zeros_like(acc_s)

    lane = lax.broadcasted_iota(I32, (t, LANES), 1)
    ks = pl.multiple_of(kj * tk, tk)

    def head_q(h):
        cq = jnp.sum(jnp.where(lane == h, cq_ref[0], 0.0), axis=1, keepdims=True)
        return q_ref[0, :, h * DH:(h + 1) * DH], cq

    def softmax_step(scores, values, m, l, acc):
        s = jnp.concatenate(scores, axis=0)
        m_new = jnp.maximum(m, jnp.max(s, axis=1, keepdims=True))
        p = jnp.exp2(s - m_new)
        alpha = jnp.exp2(m - m_new)
        pb = p.astype(BF16)
        pv = jnp.concatenate([jnp.dot(pb[h * t:(h + 1) * t], values[h], preferred_element_type=F32)
                              for h in range(n_heads)], axis=0)
        return m_new, alpha * l + jnp.sum(p, axis=1, keepdims=True), alpha * acc + pv

    scores, values = [], []
    for h in range(n_heads):
        q, cq = head_q(h)
        k = ck_ref[pl.ds(h, tk, stride=n_heads), :].astype(BF16)
        s = lax.dot_general(q, k, (((1,), (1,)), ((), ())), preferred_element_type=F32)
        scores.append(s + (cq - cr_ref[0, h, :, pl.ds(ks, tk)]))
        values.append(cv_ref[pl.ds(h, tk, stride=n_heads), :].astype(BF16))
    m, l, acc = softmax_step(scores, values, m_s[...], l_s[...], acc_s[...])
    m_s[...] = m
    l_s[...] = l
    acc_s[...] = acc

    @pl.when(kj == last)
    def _():
        r = lax.broadcasted_iota(I32, (t, t), 0)
        c = lax.broadcasted_iota(I32, (t, t), 1)
        scores, values = [], []
        for h in range(n_heads):
            sl = slice(h * DH, (h + 1) * DH)
            q, cq = head_q(h)
            s = lax.dot_general(q, kn_ref[0, :, sl], (((1,), (1,)), ((), ())), preferred_element_type=F32)
            scores.append(jnp.where(c <= r, s + (cq - cr_ref[0, h, :, past:past + t]), -jnp.inf))
            values.append(vn_ref[0, :, sl])
        _, l2, acc2 = softmax_step(scores, values, m, l, acc)
        out = acc2 / l2
        for h in range(n_heads):
            o_ref[0, :, h * DH:(h + 1) * DH] = out[h * t:(h + 1) * t].astype(BF16)


def _fox_sample(q, cache_k, cache_v, layer, k_new, v_new, c_pad, c_rows):
    b, t, hd = q.shape
    n_heads = hd // DH
    past = cache_k.shape[2]
    tk = min(CACHE_TILE, past)
    assert past % tk == 0 and past % t == 0 and tk % LANES == 0
    assert cache_k.shape[1:] == (b, past, n_heads, DH)
    kern = functools.partial(_fox_sample_kernel, n_heads=n_heads, t=t, tk=tk, past=past)
    per_b = lambda bi, kj: (bi, 0, 0)
    n_kb = past // tk
    cache_rows = lambda bi, kj: ((layer * b + bi) * n_kb + kj, 0)
    return pl.pallas_call(
        kern,
        grid=(b, n_kb),
        in_specs=[
            pl.BlockSpec((1, t, hd), per_b),
            pl.BlockSpec((tk * n_heads, DH), cache_rows),
            pl.BlockSpec((tk * n_heads, DH), cache_rows),
            pl.BlockSpec((1, t, hd), per_b),
            pl.BlockSpec((1, t, hd), per_b),
            pl.BlockSpec((1, t, LANES), lambda bi, kj: (bi, past // t, 0)),
            pl.BlockSpec((1, n_heads, 1, past + t), lambda bi, kj: (bi, 0, 0, 0)),
        ],
        out_specs=pl.BlockSpec((1, t, hd), per_b),
        out_shape=jax.ShapeDtypeStruct((b, t, hd), BF16),
        scratch_shapes=[
            pltpu.VMEM((n_heads * t, 1), F32),
            pltpu.VMEM((n_heads * t, 1), F32),
            pltpu.VMEM((n_heads * t, DH), F32),
        ],
        compiler_params=_cparams("arbitrary", "arbitrary"),
        name="fox_sample",
    )(q, cache_k.reshape(-1, DH), cache_v.reshape(-1, DH), k_new, v_new, c_pad, c_rows)


def _out_proj_kernel(a_ref, b_ref, wa_ref, wb_ref, r_ref, o_ref):
    o_ref[...] = (r_ref[...]
                  + jnp.dot(a_ref[...], wa_ref[...], preferred_element_type=F32)
                  + jnp.dot(b_ref[...], wb_ref[...], preferred_element_type=F32))


def _out_proj(a, b, w, res):
    t, ka = a.shape
    kb = b.shape[1]
    d = w.shape[1]
    tm = _row_tile(t, t)
    res = _view(res)
    assert ka == kb and res.n == t and res.row0 % tm == 0
    return pl.pallas_call(
        _out_proj_kernel,
        grid=(t // tm,),
        in_specs=[
            pl.BlockSpec((tm, ka), lambda i: (i, 0)),
            pl.BlockSpec((tm, kb), lambda i: (i, 0)),
            pl.BlockSpec((ka, d), lambda i: (0, 0)),
            pl.BlockSpec((kb, d), lambda i: (1, 0)),
            pl.BlockSpec((tm, d), lambda i: (i + res.row0 // tm, 0)),
        ],
        out_specs=pl.BlockSpec((tm, d), lambda i: (i, 0)),
        out_shape=jax.ShapeDtypeStruct((t, d), F32),
        compiler_params=_cparams("arbitrary"),
        name="out_proj",
    )(a, b, w, w, res.arr)


def _odd_in_kernel(x_ref, g_ref, w_ref, wc_ref, lg_ref, lb_ref, vg_ref, ws_ref, bs_ref, prev_ref,
                   yc_ref, yd_ref, vn_ref, st_ref,
                   xn_s, val_s, u_s, ext_s, conv_s, *, nb, lt, tiles_per_seq, taps, n_groups, lc):
    i = pl.program_id(0)
    j = pl.program_id(1)
    hist = taps - 1
    halo = ((hist + 7) // 8) * 8

    @pl.when(j == 0)
    def _():
        xn_s[...] = _rms(x_ref[...], g_ref[...]).astype(BF16)

    acc = jnp.dot(xn_s[...], w_ref[...], preferred_element_type=F32)

    @pl.when(j == 0)
    def _():
        val_s[...] = acc

    @pl.when(j == 1)
    def _():
        glu = val_s[...] * jax.nn.sigmoid(acc)
        for n in range(nb):
            ext_s[n, halo:halo + lt, :] = glu[n * lt:(n + 1) * lt, :]

        @pl.when(i % tiles_per_seq == 0)
        def _():
            for n in range(nb):
                ext_s[n, halo - hist:halo, :] = prev_ref[n]
                if halo > hist:
                    ext_s[n, 0:halo - hist, :] = jnp.zeros((halo - hist, ext_s.shape[2]), F32)

        dc = conv_s.shape[1]
        off0 = halo - hist
        win = ((off0 + hist + CONV_ROWS + 7) // 8) * 8
        for n in range(nb):
            for r0 in range(0, lt, CONV_ROWS):
                for c0 in range(0, dc, CONV_COLS):
                    cols = slice(c0, c0 + CONV_COLS)
                    blk = ext_s[n, r0:r0 + win, cols]
                    part = None
                    for b in range(8):
                        rolled = blk if b == 0 else pltpu.roll(blk, win - b, axis=0)
                        for a in range(win // 8):
                            s = 8 * a + b - off0
                            if 0 <= s < taps and 8 * a + CONV_ROWS <= win:
                                term = wc_ref[s:s + 1, cols] * rolled[8 * a:8 * a + CONV_ROWS, :]
                                part = term if part is None else part + term
                    conv_s[n * lt + r0:n * lt + r0 + CONV_ROWS, cols] = part
        for n in range(nb):
            conv = conv_s[n * lt:(n + 1) * lt, :]
            mu = jnp.mean(conv, axis=-1, keepdims=True)
            xc = conv - mu
            var = jnp.mean(xc * xc, axis=-1, keepdims=True)
            y = xc * lax.rsqrt(var + EPS) * lg_ref[...] + lb_ref[...]
            yc_ref[n * lt:(n + 1) * lt, :] = (y * jax.nn.sigmoid(y)).astype(BF16)
            tail = ext_s[n, halo - hist + lt:halo + lt, :]
            st_ref[n] = tail
            ext_s[n, halo - hist:halo, :] = tail

    @pl.when(j == 2)
    def _():
        u_s[...] = acc

    @pl.when(j == 3)
    def _():
        vn = _rms(acc, vg_ref[...])
        vn_ref[...] = vn
        vb = vn.astype(BF16)
        gw = vn.shape[1] // n_groups
        for c in range(vn.shape[0] // lc):
            rows = slice(c * lc, (c + 1) * lc)
            for g in range(n_groups):
                cols = slice(g * gw, (g + 1) * gw)
                gate = jnp.dot(ws_ref[g], vb[rows, cols], preferred_element_type=F32) + bs_ref[g]
                yd_ref[rows, cols] = (u_s[rows, cols] * gate).astype(BF16)


def _odd_in(x, g, w_in, w_conv, ln_g, ln_b, v_gain, w_s, b_s, prev, seq_len):
    x = _view(x)
    t, d = x.n, x.arr.shape[1]
    taps, dc = w_conv.shape
    assert w_in.shape[1] == 4 * dc
    n_groups = w_s.shape[0]
    tm = _row_tile(t, seq_len, cap=ROW_TILE if seq_len >= ROW_TILE else ROW_TILE // 2)
    lt = min(seq_len, tm)
    nb = tm // lt
    tiles_per_seq = seq_len // lt
    n_seq = t // seq_len
    lc = min(CHUNK_MLP, seq_len)
    assert lt % lc == 0 and lt % CONV_ROWS == 0 and dc % CONV_COLS == 0
    hist = taps - 1
    halo = ((hist + 7) // 8) * 8
    kern = functools.partial(_odd_in_kernel, nb=nb, lt=lt, tiles_per_seq=tiles_per_seq, taps=taps,
                             n_groups=n_groups, lc=lc)
    row = lambda i, j: (i, 0)
    const = lambda i, j: (0, 0)
    const3 = lambda i, j: (0, 0, 0)
    seq3 = lambda i, j: (i // tiles_per_seq, 0, 0)
    return pl.pallas_call(
        kern,
        grid=(t // tm, 4),
        in_specs=[
            pl.BlockSpec((tm, d), lambda i, j: (i + x.row0 // tm, 0)),
            pl.BlockSpec((1, d), const),
            pl.BlockSpec((d, dc), lambda i, j: (0, j)),
            pl.BlockSpec((taps, dc), const),
            pl.BlockSpec((1, dc), const),
            pl.BlockSpec((1, dc), const),
            pl.BlockSpec((1, dc), const),
            pl.BlockSpec((n_groups, lc, lc), const3),
            pl.BlockSpec((n_groups, lc, 1), const3),
            pl.BlockSpec((nb, hist, dc), seq3),
        ],
        out_specs=[
            pl.BlockSpec((tm, dc), row),
            pl.BlockSpec((tm, dc), row),
            pl.BlockSpec((tm, dc), row),
            pl.BlockSpec((nb, hist, dc), seq3),
        ],
        out_shape=[
            jax.ShapeDtypeStruct((t, dc), BF16),
            jax.ShapeDtypeStruct((t, dc), BF16),
            jax.ShapeDtypeStruct((t, dc), F32),
            jax.ShapeDtypeStruct((n_seq, hist, dc), F32),
        ],
        scratch_shapes=[
            pltpu.VMEM((tm, d), BF16),
            pltpu.VMEM((tm, dc), F32),
            pltpu.VMEM((tm, dc), F32),
            pltpu.VMEM((nb, halo + lt, dc), F32),
            pltpu.VMEM((tm, dc), F32),
        ],
        compiler_params=_cparams("arbitrary", "arbitrary"),
        name="odd_in",
    )(x.arr, g, w_in, w_conv, ln_g, ln_b, v_gain, w_s, b_s, prev)


def _mem_kv_kernel(m_ref, g_ref, wk_ref, wv_ref, kg_ref, k_ref, v_ref, *, n_heads):
    mn = _rms(m_ref[...], g_ref[...]).astype(BF16)
    k = jnp.dot(mn, wk_ref[...], preferred_element_type=F32)
    for h in range(n_heads):
        seg = k[:, h * DH:(h + 1) * DH]
        k_ref[:, h * DH:(h + 1) * DH] = seg * lax.rsqrt(jnp.mean(seg * seg, axis=-1, keepdims=True) + EPS) * kg_ref[...]
    v_ref[...] = jnp.dot(mn, wv_ref[...], preferred_element_type=F32)


def _mem_kv(mem, g, wk, wv, k_gain):
    t, d = mem.shape
    dx = wk.shape[1]
    tm = _row_tile(t, t)
    return pl.pallas_call(
        functools.partial(_mem_kv_kernel, n_heads=dx // DH),
        grid=(t // tm,),
        in_specs=[
            pl.BlockSpec((tm, d), lambda i: (i, 0)),
            pl.BlockSpec((1, d), lambda i: (0, 0)),
            pl.BlockSpec((d, dx), lambda i: (0, 0)),
            pl.BlockSpec((d, dx), lambda i: (0, 0)),
            pl.BlockSpec((1, DH), lambda i: (0, 0)),
        ],
        out_specs=[pl.BlockSpec((tm, dx), lambda i: (i, 0))] * 2,
        out_shape=[jax.ShapeDtypeStruct((t, dx), F32)] * 2,
        compiler_params=_cparams("arbitrary"),
        name="mem_kv",
    )(mem, g, wk, wv, k_gain)


def _mem_attn_kernel(x_ref, g_ref, wq_ref, qg_ref, k_ref, v_ref, wo_ref, o_ref, att_s, *, nb, lt, n_heads):
    x = x_ref[...]
    xn = _rms(x, g_ref[...]).astype(BF16)
    q = jnp.dot(xn, wq_ref[...], preferred_element_type=F32)
    for h in range(n_heads):
        sl = slice(h * DH, (h + 1) * DH)
        seg = q[:, sl]
        qn = seg * lax.rsqrt(jnp.mean(seg * seg, axis=-1, keepdims=True) + EPS) * qg_ref[...]
        qb = (qn * (DH ** -0.5)).astype(BF16)
        for n in range(nb):
            rows = slice(n * lt, (n + 1) * lt)
            k = k_ref[n, :, sl].astype(BF16)
            v = v_ref[n, :, sl].astype(BF16)
            s = lax.dot_general(qb[rows], k, (((1,), (1,)), ((), ())), preferred_element_type=F32)
            p = jnp.exp(s - jnp.max(s, axis=1, keepdims=True))
            o = jnp.dot(p.astype(BF16), v, preferred_element_type=F32) / jnp.sum(p, axis=1, keepdims=True)
            att_s[rows, sl] = o.astype(BF16)
    o_ref[...] = x + jnp.dot(att_s[...], wo_ref[...], preferred_element_type=F32)


def _mem_attn(x, g, wq, q_gain, k, v, wo, seq_len):
    t, d = x.shape
    dx = wq.shape[1]
    n_mem = k.shape[1]
    tm = _row_tile(t, seq_len, cap=256 if seq_len < 256 else ROW_TILE)
    lt = min(seq_len, tm)
    nb = tm // lt
    tiles_per_seq = seq_len // lt
    kern = functools.partial(_mem_attn_kernel, nb=nb, lt=lt, n_heads=dx // DH)
    return pl.pallas_call(
        kern,
        grid=(t // tm,),
        in_specs=[
            pl.BlockSpec((tm, d), lambda i: (i, 0)),
            pl.BlockSpec((1, d), lambda i: (0, 0)),
            pl.BlockSpec((d, dx), lambda i: (0, 0)),
            pl.BlockSpec((1, DH), lambda i: (0, 0)),
            pl.BlockSpec((nb, n_mem, dx), lambda i: (i // tiles_per_seq, 0, 0)),
            pl.BlockSpec((nb, n_mem, dx), lambda i: (i // tiles_per_seq, 0, 0)),
            pl.BlockSpec((dx, d), lambda i: (0, 0)),
        ],
        out_specs=pl.BlockSpec((tm, d), lambda i: (i, 0)),
        out_shape=jax.ShapeDtypeStruct((t, d), F32),
        scratch_shapes=[pltpu.VMEM((tm, dx), BF16)],
        compiler_params=_cparams("arbitrary"),
        name="mem_attn",
    )(x, g, wq, q_gain, k, v, wo)


GROUP_LANE0 = N_EXPERTS
PAIRS_PER_GROUP = EXP_PER_GROUP * (EXP_PER_GROUP - 1) // 2
N_CLASSES = N_GROUPS * PAIRS_PER_GROUP
META_C, META_R, META_G = 0, 1, 2
MOE_BLOCK = 256


def _router_kernel(x_ref, g_ref, w_ref, b_ref, meta_ref, cnt_ref, *, tm):
    @pl.when(pl.program_id(0) == 0)
    def _():
        cnt_ref[...] = jnp.zeros_like(cnt_ref)

    xn = _rms(x_ref[...], g_ref[...]).astype(BF16)
    logits = jnp.dot(xn, w_ref[...], preferred_element_type=F32) + b_ref[...]
    lane = lax.broadcasted_iota(I32, (tm, LANES), 1)
    big = jnp.int32(LANES)

    def first_argmax(vals):
        top = jnp.max(vals, axis=1, keepdims=True)
        return top, jnp.min(jnp.where(vals == top, lane, big), axis=1, keepdims=True)

    is_grp = (lane >= GROUP_LANE0) & (lane < GROUP_LANE0 + N_GROUPS)
    lg = jnp.where(is_grp, logits, -jnp.inf)
    g_top, g_lane = first_argmax(lg)
    p_sel = 1.0 / jnp.sum(jnp.where(is_grp, jnp.exp(lg - g_top), 0.0), axis=1, keepdims=True)
    e0 = (g_lane - GROUP_LANE0) * EXP_PER_GROUP
    le = jnp.where((lane >= e0) & (lane < e0 + EXP_PER_GROUP), logits, -jnp.inf)
    v1, i1 = first_argmax(le)
    v2, i2 = first_argmax(jnp.where(lane == i1, -jnp.inf, le))
    e = jnp.exp(v2 - v1)
    g1 = p_sel / (1.0 + e)
    g2 = p_sel * e / (1.0 + e)

    swap = i2 < i1
    lo = jnp.minimum(i1, i2) - e0
    hi = jnp.maximum(i1, i2) - e0
    pair = jnp.right_shift(lo * (2 * EXP_PER_GROUP - 1 - lo), 1) + (hi - lo - 1)
    cls = (g_lane - GROUP_LANE0) * PAIRS_PER_GROUP + pair
    g_lo = jnp.where(swap, g2, g1)
    g_hi = jnp.where(swap, g1, g2)

    sel = lane == cls
    onehot = jnp.where(sel, 1.0, 0.0)
    r = lax.broadcasted_iota(I32, (tm, tm), 0)
    c = lax.broadcasted_iota(I32, (tm, tm), 1)
    earlier = jnp.where(c < r, 1.0, 0.0).astype(BF16)
    before = jnp.dot(earlier, onehot.astype(BF16), preferred_element_type=F32) + cnt_ref[...]
    rank = jnp.sum(jnp.where(sel, before, 0.0), axis=1, keepdims=True)
    cnt_ref[...] = cnt_ref[...] + jnp.sum(onehot, axis=0, keepdims=True)

    meta = jnp.zeros((tm, LANES), F32)
    for k, val in ((META_C, cls.astype(F32)), (META_R, rank), (META_G, g_lo), (META_G + 1, g_hi)):
        meta = jnp.where(lane == k, val, meta)
    meta_ref[...] = meta


def _router(x, g, w_rt, b_rt):
    t, d = x.shape
    tm = ROW_TILE if t % ROW_TILE == 0 else GATHER_TILE
    assert t % tm == 0
    return pl.pallas_call(
        functools.partial(_router_kernel, tm=tm),
        grid=(t // tm,),
        in_specs=[
            pl.BlockSpec((tm, d), lambda i: (i, 0)),
            pl.BlockSpec((1, d), lambda i: (0, 0)),
            pl.BlockSpec((d, LANES), lambda i: (0, 0)),
            pl.BlockSpec((1, LANES), lambda i: (0, 0)),
        ],
        out_specs=[pl.BlockSpec((tm, LANES), lambda i: (i, 0)), pl.BlockSpec((1, LANES), lambda i: (0, 0))],
        out_shape=[jax.ShapeDtypeStruct((t, LANES), F32), jax.ShapeDtypeStruct((1, LANES), F32)],
        compiler_params=_cparams("arbitrary"),
        name="moe_router",
    )(x, g, w_rt, b_rt)


def _row_copy(src, src_row, dst, dst_row, sem):
    return pltpu.make_async_copy(src.at[pl.ds(src_row, 1)], dst.at[pl.ds(dst_row, 1)], sem)


def _dispatch_kernel(pos_ref, x_ref, g_ref, meta_ref, zero_ref, xb_ref, xn_s, sem, *, tm, d):
    del zero_ref
    i = pl.program_id(0)
    n_steps = pl.num_programs(0)
    slot = i % 2
    base = i * tm

    def drain(sl):
        for _ in range(tm):
            _row_copy(xn_s.at[sl], 0, xb_ref, 0, sem.at[sl]).wait()

    @pl.when(i >= 2)
    def _():
        drain(slot)

    xn_s[slot, :, :d] = _rms(x_ref[...], g_ref[...])
    xn_s[slot, :, d:] = meta_ref[...]

    def issue(r, _):
        _row_copy(xn_s.at[slot], r, xb_ref, pos_ref[base + r], sem.at[slot]).start()
        return 0

    lax.fori_loop(0, tm, issue, 0, unroll=COPY_UNROLL)

    @pl.when(i == n_steps - 1)
    def _():
        @pl.when(n_steps >= 2)
        def _():
            drain(1 - slot)

        drain(slot)


def _dispatch(pos, x, g, meta, n_rows):
    t, d = x.shape
    tm = min(GATHER_TILE, t)
    assert t % tm == 0
    zeros = jnp.zeros((n_rows, d + LANES), F32)
    return pl.pallas_call(
        functools.partial(_dispatch_kernel, tm=tm, d=d),
        grid_spec=pltpu.PrefetchScalarGridSpec(
            num_scalar_prefetch=1,
            grid=(t // tm,),
            in_specs=[
                pl.BlockSpec((tm, d), lambda i, pos: (i, 0)),
                pl.BlockSpec((1, d), lambda i, pos: (0, 0)),
                pl.BlockSpec((tm, LANES), lambda i, pos: (i, 0)),
                pl.BlockSpec(memory_space=pl.ANY),
            ],
            out_specs=pl.BlockSpec(memory_space=pl.ANY),
            scratch_shapes=[pltpu.VMEM((2, tm, d + LANES), F32), pltpu.SemaphoreType.DMA((2,))],
        ),
        out_shape=jax.ShapeDtypeStruct((n_rows, d + LANES), F32),
        input_output_aliases={4: 0},
        compiler_params=_cparams("arbitrary"),
        name="moe_dispatch",
    )(pos, x, g, meta, zeros)


def _expert_kernel(ba_ref, bb_ref, nu_ref, x_ref, wga_ref, wua_ref, wda_ref, wgb_ref, wub_ref, wdb_ref,
                   y_ref, *, d):
    del ba_ref, bb_ref
    used = pl.program_id(0) < nu_ref[0]

    @pl.when(used)
    def _():
        x = x_ref[:, :d].astype(BF16)
        rec = x_ref[:, d:]

        def ffn(wg_ref, wu_ref, wd_ref):
            gate = jnp.dot(x, wg_ref[0, 0], preferred_element_type=F32)
            up = jnp.dot(x, wu_ref[0, 0], preferred_element_type=F32)
            hid = (gate * jax.nn.sigmoid(gate) * up).astype(BF16)
            return jnp.dot(hid, wd_ref[0, 0], preferred_element_type=F32)

        y_ref[...] = (rec[:, META_G:META_G + 1] * ffn(wga_ref, wua_ref, wda_ref)
                      + rec[:, META_G + 1:META_G + 2] * ffn(wgb_ref, wub_ref, wdb_ref))

    @pl.when(jnp.logical_not(used))
    def _():
        y_ref[...] = jnp.zeros_like(y_ref)


def _experts(blk_a, blk_b, n_used, xb, w_gate, w_up, w_down, layer, bm):
    de, d = w_down.shape[2:]
    p = xb.shape[0]
    n_blocks = p // bm

    def xmap(b, ba, bb, nu):
        return (jnp.minimum(b, jnp.maximum(nu[0] - 1, 0)), 0)

    def amap(b, ba, bb, nu):
        return (layer, ba[b], 0, 0)

    def bmap(b, ba, bb, nu):
        return (layer, bb[b], 0, 0)

    return pl.pallas_call(
        functools.partial(_expert_kernel, d=d),
        grid_spec=pltpu.PrefetchScalarGridSpec(
            num_scalar_prefetch=3,
            grid=(n_blocks,),
            in_specs=[
                pl.BlockSpec((bm, d + LANES), xmap),
                pl.BlockSpec((1, 1, d, de), amap),
                pl.BlockSpec((1, 1, d, de), amap),
                pl.BlockSpec((1, 1, de, d), amap),
                pl.BlockSpec((1, 1, d, de), bmap),
                pl.BlockSpec((1, 1, d, de), bmap),
                pl.BlockSpec((1, 1, de, d), bmap),
            ],
            out_specs=pl.BlockSpec((bm, d), lambda b, ba, bb, nu: (b, 0)),
        ),
        out_shape=jax.ShapeDtypeStruct((p, d), F32),
        compiler_params=_cparams("arbitrary"),
        name="moe_experts",
    )(blk_a, blk_b, n_used, xb, w_gate, w_up, w_down, w_gate, w_up, w_down)


def _combine_kernel(pos_ref, x_ref, yb_ref, *refs, tm, head_tiles):
    *o_refs, buf_s, sem = refs
    i = pl.program_id(0)
    slot = i % 2

    def issue(tile, sl):
        base = tile * tm

        def start_row(r, _):
            _row_copy(yb_ref, pos_ref[base + r], buf_s.at[sl], r, sem.at[sl]).start()
            return 0

        lax.fori_loop(0, tm, start_row, 0, unroll=COPY_UNROLL)

    @pl.when(i == 0)
    def _():
        issue(0, 0)

    @pl.when(i + 1 < pl.num_programs(0))
    def _():
        issue(i + 1, 1 - slot)

    for _ in range(tm):
        _row_copy(yb_ref, 0, buf_s.at[slot], 0, sem.at[slot]).wait()
    out = x_ref[...] + buf_s[slot]
    if head_tiles is None:
        o_refs[0][...] = out
    else:
        @pl.when(i < head_tiles)
        def _():
            o_refs[0][...] = out

        @pl.when(i >= head_tiles)
        def _():
            o_refs[1][...] = out


def _combine(pos, x, yb, split_at=None):
    t, d = x.shape
    tm = min(GATHER_TILE, t)
    if split_at is None:
        head_tiles = None
        out_specs = pl.BlockSpec((tm, d), lambda i, pos: (i, 0))
        out_shape = jax.ShapeDtypeStruct((t, d), F32)
    else:
        assert split_at % tm == 0 and 0 < split_at < t
        head_tiles = split_at // tm
        out_specs = [pl.BlockSpec((tm, d), lambda i, pos: (jnp.minimum(i, head_tiles - 1), 0)),
                     pl.BlockSpec((tm, d), lambda i, pos: (jnp.maximum(i - head_tiles, 0), 0))]
        out_shape = [jax.ShapeDtypeStruct((split_at, d), F32), jax.ShapeDtypeStruct((t - split_at, d), F32)]
    return pl.pallas_call(
        functools.partial(_combine_kernel, tm=tm, head_tiles=head_tiles),
        grid_spec=pltpu.PrefetchScalarGridSpec(
            num_scalar_prefetch=1,
            grid=(t // tm,),
            in_specs=[
                pl.BlockSpec((tm, d), lambda i, pos: (i, 0)),
                pl.BlockSpec(memory_space=pl.ANY),
            ],
            out_specs=out_specs,
            scratch_shapes=[pltpu.VMEM((2, tm, d), F32), pltpu.SemaphoreType.DMA((2,))],
        ),
        out_shape=out_shape,
        compiler_params=_cparams("arbitrary"),
        name="moe_combine",
    )(pos, x, yb)


def _moe(x, g, w_rt, b_rt, w_gate, w_up, w_down, layer, split_at=None):
    t, d = x.shape
    meta, cnt = _router(x, g, w_rt, b_rt)
    bm = MOE_BLOCK
    counts = cnt[0, :N_CLASSES].astype(I32)
    padded = ((counts + bm - 1) // bm) * bm
    pend = jnp.cumsum(padded)
    pstart = pend - padded
    n_blocks = (t + N_CLASSES * (bm - 1) + bm - 1) // bm
    cls = meta[:, META_C].astype(I32)
    rank = meta[:, META_R].astype(I32)
    onehot = cls[:, None] == jnp.arange(N_CLASSES, dtype=I32)[None, :]
    pos = jnp.sum(jnp.where(onehot, pstart[None, :], 0), axis=-1) + rank
    blk_row0 = jnp.arange(n_blocks, dtype=I32) * bm
    blk_cls = jnp.minimum(jnp.sum((pend[None, :] <= blk_row0[:, None]).astype(I32), axis=1), N_CLASSES - 1)
    pairs = [(grp * EXP_PER_GROUP + a, grp * EXP_PER_GROUP + b) for grp in range(N_GROUPS)
             for a in range(EXP_PER_GROUP) for b in range(a + 1, EXP_PER_GROUP)]
    blk_a = jnp.asarray([p[0] for p in pairs], I32)[blk_cls]
    blk_b = jnp.asarray([p[1] for p in pairs], I32)[blk_cls]
    n_used = (pend[-1:] // bm).astype(I32)
    xb = _dispatch(pos, x, g, meta, n_blocks * bm)
    yb = _experts(blk_a, blk_b, n_used, xb, w_gate, w_up, w_down, layer, bm)
    return _combine(pos, x, yb, split_at)


def _pad_lanes(a):
    return jnp.pad(a, [(0, 0)] * (a.ndim - 1) + [(0, LANES - a.shape[-1])])


def kernel(x_prompt, x_sample, mem_prompt, cache_fox_k, cache_fox_v, cache_fox_logf, state_conv_a, state_conv_c, cache_mem_k, cache_mem_v, norm_mix, norm_xmem, norm_mem, norm_ffn, even_w_in, even_w_conv_a, fox_q_gain, fox_k_gain, fox_f_bias, even_w_out, odd_w_in, odd_w_conv_c, conf_ln_gain, conf_ln_bias, sgu_v_gain, sgu_w, sgu_b, odd_w_out, xmem_wq, xmem_wk, xmem_wv, xmem_q_gain, xmem_k_gain, xmem_wo, moe_w_group, moe_b_group, moe_w_router, moe_b_router, moe_w_gate, moe_w_up, moe_w_down):
    bp, sp, d = x_prompt.shape
    bs, ss, _ = x_sample.shape
    depth = norm_mix.shape[0]
    n_mem = mem_prompt.shape[1]
    d_a = even_w_conv_a.shape[2]
    conv_a = even_w_conv_a.shape[1]
    conv_c = odd_w_conv_c.shape[1]
    d_c = odd_w_conv_c.shape[2]
    h_b = fox_f_bias.shape[1]
    d_b = h_b * DH
    d_x = xmem_wq.shape[2]
    h_mem = d_x // DH
    n_main = 3 * d_a + 3 * d_b

    xp = x_prompt.reshape(bp * sp, d)
    xs = x_sample.reshape(bs * ss, d)
    mem2 = mem_prompt.reshape(bp * n_mem, d)
    moe_w = (moe_w_gate.astype(BF16), moe_w_up.astype(BF16), moe_w_down.astype(BF16))
    outs = {k: [] for k in ('fk_p', 'fv_p', 'fl_p', 'fk_s', 'fv_s', 'fl_s', 'ca_p', 'ca_s', 'cc_p', 'cc_s', 'cv_s', 'mk_p', 'mv_p')}

    for i in range(depth):
        g_mix = norm_mix[i][None]
        if i % 2 == 0:
            e = i // 2
            w_in = even_w_in[e]
            w_main = w_in[:, :n_main].astype(BF16)
            w_f = _pad_lanes(w_in[:, n_main:]).astype(BF16)
            f_bias = _pad_lanes(fox_f_bias[e][None])
            w_out = even_w_out[e].astype(BF16)
            args = (g_mix, w_main, w_f, f_bias, even_w_conv_a[e], fox_q_gain[e][None], fox_k_gain[e][None])

            ya, q, k, kb, v, vb, lf, st = _even_in(xp, *args, jnp.zeros((bp, conv_a - 1, d_a), F32), sp)
            lf3 = lf.reshape(bp, sp, LANES)
            c = _cumsum_time(lf3)
            c_rows = jnp.swapaxes(c[:, :, :h_b], 1, 2)[:, :, None, :]
            yb = _fox_prompt(q.reshape(bp, sp, d_b), kb.reshape(bp, sp, d_b), vb.reshape(bp, sp, d_b), c, c_rows)
            xp = _out_proj(ya, yb.reshape(bp * sp, d_b), w_out, xp)
            outs['fk_p'].append(k.reshape(bp, sp, h_b, DH))
            outs['fv_p'].append(v.reshape(bp, sp, h_b, DH))
            outs['fl_p'].append(lf3[:, :, :h_b])
            outs['ca_p'].append(st)

            ya, q, k, kb, v, vb, lf, st = _even_in(xs, *args, state_conv_a[e], ss)
            lf3 = lf.reshape(bs, ss, LANES)
            lf_all = jnp.concatenate([_pad_lanes(cache_fox_logf[e]), lf3], axis=1)
            c = _cumsum_time(lf_all)
            c_rows = jnp.swapaxes(c[:, :, :h_b], 1, 2)[:, :, None, :]
            yb = _fox_sample(q.reshape(bs, ss, d_b), cache_fox_k, cache_fox_v, e, kb.reshape(bs, ss, d_b),
                             vb.reshape(bs, ss, d_b), c, c_rows)
            xs = _out_proj(ya, yb.reshape(bs * ss, d_b), w_out, xs)
            outs['fk_s'].append(k.reshape(bs, ss, h_b, DH))
            outs['fv_s'].append(v.reshape(bs, ss, h_b, DH))
            outs['fl_s'].append(lf3[:, :, :h_b])
            outs['ca_s'].append(st)
        else:
            o = i // 2
            w_in = odd_w_in[o].astype(BF16)
            w_out = odd_w_out[o].astype(BF16)
            tril = jnp.tril(jnp.ones(sgu_w.shape[2:], dtype=bool))
            w_s = jnp.where(tril[None], sgu_w[o], 0).astype(BF16)
            b_s = sgu_b[o][:, :, None]

            def odd(x, prev, seq_len):
                lc = min(CHUNK_MLP, seq_len)
                return _odd_in(x, g_mix, w_in, odd_w_conv_c[o], conf_ln_gain[o][None], conf_ln_bias[o][None],
                               sgu_v_gain[o][None], w_s[:, :lc, :lc], b_s[:, :lc], prev, seq_len)

            yc, yd, _, st = odd(xp, jnp.zeros((bp, conv_c - 1, d_c), F32), sp)
            xp = _out_proj(yc, yd, w_out, xp)
            outs['cc_p'].append(st)
            yc, yd, vn, st = odd(xs, state_conv_c[o], ss)
            xs = _out_proj(yc, yd, w_out, xs)
            outs['cc_s'].append(st)
            outs['cv_s'].append(vn.reshape(bs, ss, -1))

        g_x = norm_xmem[i][None]
        wq = xmem_wq[i].astype(BF16)
        wo = xmem_wo[i].astype(BF16)
        q_gain = xmem_q_gain[i][None]
        mk, mv = _mem_kv(mem2, norm_mem[i][None], xmem_wk[i].astype(BF16), xmem_wv[i].astype(BF16), xmem_k_gain[i][None])
        mk3 = mk.reshape(bp, n_mem, d_x)
        mv3 = mv.reshape(bp, n_mem, d_x)
        outs['mk_p'].append(mk3.reshape(bp, n_mem, h_mem, DH))
        outs['mv_p'].append(mv3.reshape(bp, n_mem, h_mem, DH))
        xp = _mem_attn(xp, g_x, wq, q_gain, mk3, mv3, wo, sp)
        xs = _mem_attn(xs, g_x, wq, q_gain, cache_mem_k[i].reshape(bs, n_mem, d_x),
                       cache_mem_v[i].reshape(bs, n_mem, d_x), wo, ss)

        g_f = norm_ffn[i][None]
        w_rt = _pad_lanes(jnp.concatenate([moe_w_router[i], moe_w_group[i]], axis=1)).astype(BF16)
        b_rt = _pad_lanes(jnp.concatenate([moe_b_router[i], moe_b_group[i]])[None])
        x_in = jnp.concatenate([xp, xs], axis=0)
        if i + 1 < depth:
            x_all = _moe(x_in, g_f, w_rt, b_rt, *moe_w, i)
            xp = _Rows(x_all, 0, bp * sp)
            xs = _Rows(x_all, bp * sp, bs * ss)
        else:
            xp, xs = _moe(x_in, g_f, w_rt, b_rt, *moe_w, i, split_at=bp * sp)

    st = lambda key: jnp.stack(outs[key])
    return (xp.reshape(bp, sp, d), xs.reshape(bs, ss, d),
            st('fk_p'), st('fv_p'), st('fl_p'),
            st('fk_s'), st('fv_s'), st('fl_s'),
            st('ca_p'), st('ca_s'),
            st('cc_p'), st('cc_s'),
            st('cv_s'),
            st('mk_p'), st('mv_p'))
```
